```python
import math
import jax, jax.numpy as jnp
from jax import lax
import numpy as np

D_MODEL = 4096
BATCH = 2
SEQ = 8192
DEPTH = 2

HG_DK = 128
HG_DV = 128
HG_WIDTH = D_MODEL // 4
HG_HEADS = HG_WIDTH // HG_DV
HG_CHUNK = 64
DA_DH = 128
DA_HEADS = D_MODEL // (4 * DA_DH)
DA_QK_WIDTH = DA_HEADS * 2 * DA_DH
DA_V_WIDTH = DA_HEADS * 2 * DA_DH
DA_QBLOCK = 128
ROPE_THETA = 500000.0
ROT_DIM = DA_DH // 4
SG_CHUNK = 128
SG_WIDTH = D_MODEL // 4
SG_GDIM = 128
SG_GROUPS = SG_WIDTH // SG_GDIM
N_BRANCH = 3
D_FF = 4 * D_MODEL
ALPHA = (2.0 * DEPTH) ** 0.25
BETA = (8.0 * DEPTH) ** -0.25
EPS = 1e-5
IN_SIZES = (HG_WIDTH, HG_WIDTH, HG_WIDTH, HG_WIDTH, HG_WIDTH,
            DA_QK_WIDTH, DA_QK_WIDTH, DA_V_WIDTH, SG_WIDTH, SG_WIDTH)
D_IN = sum(IN_SIZES)

kernel_name = "hybrid_gated_bidir_encoder"


def layer_norm(t, g, b):
    t32 = t.astype(jnp.float32)
    mu = jnp.mean(t32, axis=-1, keepdims=True)
    var = jnp.mean(jnp.square(t32 - mu), axis=-1, keepdims=True)
    return ((t32 - mu) * lax.rsqrt(var + EPS) * g + b).astype(t.dtype)


def rms_norm(t, g):
    t32 = t.astype(jnp.float32)
    ms = jnp.mean(jnp.square(t32), axis=-1, keepdims=True)
    return (t32 * lax.rsqrt(ms + EPS) * g).astype(t.dtype)


def rotary_tables(seq):
    pos = jnp.arange(seq, dtype=jnp.float32)
    freqs = ROPE_THETA ** (-jnp.arange(0, ROT_DIM, 2, dtype=jnp.float32) / ROT_DIM)
    ang = pos[:, None] * freqs[None, :]
    return jnp.cos(ang), jnp.sin(ang)


def partial_rotary(t, cos, sin):
    half = ROT_DIM // 2
    c = cos[None, :, None, None, :].astype(t.dtype)
    s = sin[None, :, None, None, :].astype(t.dtype)
    t1 = t[..., :half]
    t2 = t[..., half:ROT_DIM]
    return jnp.concatenate([t1 * c - t2 * s, t2 * c + t1 * s, t[..., ROT_DIM:]], axis=-1)


def hgrn_lower_bounds(lb_raw):
    p = jax.nn.softmax(lb_raw.astype(jnp.float32), axis=0)
    cum = jnp.cumsum(p, axis=0)
    return cum - cum[:1]


def hgrn_gates(f_pre, lb):
    x32 = f_pre.astype(jnp.float32)
    log_f = jnp.logaddexp(jnp.log(lb), jnp.log1p(-lb) + jax.nn.log_sigmoid(x32))
    k = (1.0 - lb) * jax.nn.sigmoid(-x32)
    return k, log_f


def gla_chunkwise(q, k, v, log_f):
    B, S, H, DK = q.shape
    DV = v.shape[-1]
    nc = S // HG_CHUNK

    def to_chunks(t):
        return t.reshape(B, nc, HG_CHUNK, H, t.shape[-1]).transpose(1, 0, 3, 2, 4)

    lower = jnp.tril(jnp.ones((HG_CHUNK, HG_CHUNK), dtype=bool))

    def step(state, blk):
        qc, kc, vc, gc = blk
        b = jnp.cumsum(gc, axis=2)
        rel = jnp.where(lower[None, None, :, :, None],
                        b[:, :, :, None, :] - b[:, :, None, :, :], -jnp.inf)
        scores = jnp.einsum('bhtd,bhsd,bhtsd->bhts', qc, kc, jnp.exp(rel))
        o = (jnp.einsum('bhts,bhsv->bhtv', scores, vc)
             + jnp.einsum('bhtd,bhdv->bhtv', qc * jnp.exp(b), state))
        b_last = b[:, :, -1:, :]
        state = (jnp.exp(b_last[:, :, 0, :])[..., None] * state
                 + jnp.einsum('bhsd,bhsv->bhdv', kc * jnp.exp(b_last - b), vc))
        return state, o

    init = jnp.zeros((B, H, DK, DV), jnp.float32)
    _, o = lax.scan(step, init, (to_chunks(q), to_chunks(k), to_chunks(v), to_chunks(log_f)))
    return o.transpose(1, 0, 3, 2, 4).reshape(B, S, H, DV)


def hgrn2_mixer(a_q, a_ff, a_fb, a_i, a_g, lb, norm_g):
    B, S, _ = a_q.shape
    heads = lambda t: t.astype(jnp.float32).reshape(B, S, HG_HEADS, -1)
    q = heads(a_q)
    v = heads(a_i)
    k_f, lf_f = hgrn_gates(a_ff, lb[0])
    k_b, lf_b = hgrn_gates(a_fb, lb[1])
    o_fwd = gla_chunkwise(q, heads(k_f), v, heads(lf_f))
    flip = lambda t: jnp.flip(t, axis=1)
    o_bwd = flip(gla_chunkwise(flip(q), flip(heads(k_b)), flip(v), flip(heads(lf_b))))
    o = rms_norm(o_fwd + o_bwd, norm_g.reshape(HG_HEADS, HG_DV))
    o = o.reshape(B, S, HG_WIDTH).astype(a_q.dtype) * jax.nn.silu(a_g)
    return o


def diff_attention(b_q, b_k, b_v, cos, sin, lam_params, norm_g, layer):
    B, S, _ = b_q.shape
    q = partial_rotary(b_q.reshape(B, S, DA_HEADS, 2, DA_DH), cos, sin) * (DA_DH ** -0.5)
    k = partial_rotary(b_k.reshape(B, S, DA_HEADS, 2, DA_DH), cos, sin)
    v = b_v.reshape(B, S, DA_HEADS, 2 * DA_DH)
    lam_init = 0.8 - 0.6 * math.exp(-0.3 * layer)
    lp = lam_params.astype(jnp.float32)
    lam = jnp.exp(jnp.sum(lp[0] * lp[1])) - jnp.exp(jnp.sum(lp[2] * lp[3])) + lam_init
    q_blocks = q.reshape(B, S // DA_QBLOCK, DA_QBLOCK, DA_HEADS, 2, DA_DH).transpose(1, 0, 2, 3, 4, 5)

    def attend(qb):
        s = jnp.einsum('bqhcd,bkhcd->bhcqk', qb, k).astype(jnp.float32)
        p = jax.nn.softmax(s, axis=-1)
        a = p[:, :, 0] - lam * p[:, :, 1]
        return jnp.einsum('bhqk,bkhv->bqhv', a.astype(v.dtype), v)

    o = lax.map(attend, q_blocks)
    o = o.transpose(1, 0, 2, 3, 4).reshape(B, S, DA_HEADS, 2 * DA_DH)
    o = rms_norm(o, norm_g) * (1.0 - lam_init)
    return o.reshape(B, S, DA_V_WIDTH)


def spatial_gating(c_u, c_v, norm_g, norm_b, w_s, b_s):
    B, S, _ = c_u.shape
    u = jax.nn.gelu(c_u)
    v = layer_norm(jax.nn.gelu(c_v), norm_g, norm_b)
    vb = v.reshape(B, S // SG_CHUNK, SG_CHUNK, SG_GROUPS, SG_GDIM)
    mixed = jnp.einsum('gts,bnsgc->bntgc', w_s, vb) + b_s.T[:, :, None]
    return u * mixed.reshape(B, S, SG_WIDTH)


def hybrid_mixer(h, layer, cos, sin, lb, w_in, hg_norm_g, da_lambda, da_norm_g,
                 sg_norm_g, sg_norm_b, sg_w_s, sg_b_s, w_branch_a, w_branch_b,
                 w_branch_c, w_gate, b_gate, w_out):
    B, S, _ = h.shape
    split_idx = []
    acc = 0
    for n in IN_SIZES[:-1]:
        acc += n
        split_idx.append(acc)
    proj = h @ w_in
    a_q, a_ff, a_fb, a_i, a_g, b_q, b_k, b_v, c_u, c_v = jnp.split(proj, split_idx, axis=-1)
    y_a = hgrn2_mixer(a_q, a_ff, a_fb, a_i, a_g, lb, hg_norm_g)
    y_b = diff_attention(b_q, b_k, b_v, cos, sin, da_lambda, da_norm_g, layer)
    y_c = spatial_gating(c_u, c_v, sg_norm_g, sg_norm_b, sg_w_s, sg_b_s)
    gates = jax.nn.sigmoid(h @ w_gate + b_gate).reshape(B, S, N_BRANCH, D_MODEL)
    merged = (gates[:, :, 0] * (y_a @ w_branch_a)
              + gates[:, :, 1] * (y_b @ w_branch_b)
              + gates[:, :, 2] * (y_c @ w_branch_c))
    return merged @ w_out


def setup_inputs(seed: int = 0) -> dict:
    key = jax.random.key(seed)
    ks = jax.random.split(key, 24)
    f32 = jnp.float32

    def normal(k, shape, scale):
        return jax.random.normal(k, shape, f32) * scale

    col_scales = (1.0, 1.0, 1.0, BETA, 1.0, 1.0, 1.0, BETA, 1.0, 1.0)
    col_scale = jnp.concatenate([jnp.full((n,), s, f32) for n, s in zip(IN_SIZES, col_scales)])
    return {
        "x": normal(ks[0], (BATCH, SEQ, D_MODEL), 1.0),
        "w_in": normal(ks[1], (DEPTH, D_MODEL, D_IN), D_MODEL ** -0.5) * col_scale,
        "hg_lb_raw": normal(ks[2], (DEPTH, 2, HG_WIDTH), 0.1),
        "hg_norm_g": 1.0 + normal(ks[3], (DEPTH, HG_WIDTH), 0.05),
        "da_lambda": normal(ks[4], (DEPTH, 4, DA_DH), 0.1),
        "da_norm_g": 1.0 + normal(ks[5], (DEPTH, 2 * DA_DH), 0.05),
        "sg_norm_g": 1.0 + normal(ks[6], (DEPTH, SG_WIDTH), 0.05),
        "sg_norm_b": normal(ks[7], (DEPTH, SG_WIDTH), 0.02),
        "sg_w_s": normal(ks[8], (DEPTH, SG_GROUPS, SG_CHUNK, SG_CHUNK), 0.5 * SG_CHUNK ** -0.5),
        "sg_b_s": 1.0 + normal(ks[9], (DEPTH, SG_GROUPS, SG_CHUNK), 0.05),
        "w_branch_a": normal(ks[10], (DEPTH, HG_WIDTH, D_MODEL), BETA * HG_WIDTH ** -0.5),
        "w_branch_b": normal(ks[11], (DEPTH, DA_V_WIDTH, D_MODEL), BETA * DA_V_WIDTH ** -0.5),
        "w_branch_c": normal(ks[12], (DEPTH, SG_WIDTH, D_MODEL), BETA * SG_WIDTH ** -0.5),
        "w_gate": normal(ks[13], (DEPTH, D_MODEL, N_BRANCH * D_MODEL), D_MODEL ** -0.5),
        "b_gate": normal(ks[14], (DEPTH, N_BRANCH * D_MODEL), 0.02),
        "w_out": normal(ks[15], (DEPTH, D_MODEL, D_MODEL), BETA * D_MODEL ** -0.5),
        "ln1_g": 1.0 + normal(ks[16], (DEPTH, D_MODEL), 0.05),
        "ln1_b": normal(ks[17], (DEPTH, D_MODEL), 0.02),
        "w_up": normal(ks[18], (DEPTH, D_MODEL, D_FF), BETA * D_MODEL ** -0.5),
        "w_down": normal(ks[19], (DEPTH, D_FF, D_MODEL), BETA * D_FF ** -0.5),
        "ln2_g": 1.0 + normal(ks[20], (DEPTH, D_MODEL), 0.05),
        "ln2_b": normal(ks[21], (DEPTH, D_MODEL), 0.02),
    }


def reference(x, w_in, hg_lb_raw, hg_norm_g, da_lambda, da_norm_g, sg_norm_g, sg_norm_b,
              sg_w_s, sg_b_s, w_branch_a, w_branch_b, w_branch_c, w_gate, b_gate, w_out,
              ln1_g, ln1_b, w_up, w_down, ln2_g, ln2_b):
    cos, sin = rotary_tables(x.shape[1])
    lb_all = hgrn_lower_bounds(hg_lb_raw)
    for layer in range(DEPTH):
        mix = hybrid_mixer(x, layer, cos, sin, lb_all[layer], w_in[layer], hg_norm_g[layer],
                           da_lambda[layer], da_norm_g[layer], sg_norm_g[layer], sg_norm_b[layer],
                           sg_w_s[layer], sg_b_s[layer], w_branch_a[layer], w_branch_b[layer],
                           w_branch_c[layer], w_gate[layer], b_gate[layer], w_out[layer])
        x = layer_norm(ALPHA * x + mix, ln1_g[layer], ln1_b[layer])
        hid = jnp.square(jax.nn.relu(x @ w_up[layer]))
        x = layer_norm(ALPHA * x + hid @ w_down[layer], ln2_g[layer], ln2_b[layer])
    return x
```

```python
import functools
import math

import jax
import jax.numpy as jnp
from jax import lax
from jax.experimental import pallas as pl
from jax.experimental.pallas import tpu as pltpu

D_MODEL = 4096
DEPTH = 2
HG_WIDTH = D_MODEL // 4
HG_DK = 128
HG_HEADS = HG_WIDTH // HG_DK
DA_DH = 128
DA_HEADS = D_MODEL // (4 * DA_DH)
DA_QK_WIDTH = DA_HEADS * 2 * DA_DH
DA_V_WIDTH = DA_HEADS * 2 * DA_DH
ROPE_THETA = 500000.0
ROT_DIM = DA_DH // 4
SG_CHUNK = 128
SG_WIDTH = D_MODEL // 4
SG_GDIM = 128
SG_GROUPS = SG_WIDTH // SG_GDIM
N_BRANCH = 3
D_FF = 4 * D_MODEL
ALPHA = (2.0 * DEPTH) ** 0.25
EPS = 1e-5
D_IN = 5 * HG_WIDTH + 2 * DA_QK_WIDTH + DA_V_WIDTH + 2 * SG_WIDTH

COL_AQ, COL_AFF, COL_AFB, COL_AI, COL_AG = (i * HG_WIDTH for i in range(5))
COL_BQ = 5 * HG_WIDTH
COL_BK = COL_BQ + DA_QK_WIDTH
COL_BV = COL_BK + DA_QK_WIDTH
COL_CU = COL_BV + DA_V_WIDTH
COL_CV = COL_CU + SG_WIDTH

LANES = 128
VMEM_LIMIT = 56 * 1024 * 1024
GLA_CHUNK = 64
GLA_SUB = 16
LN_ROWS = 64

F32 = jnp.float32
BF16 = jnp.bfloat16


def _params(sem):
    return pltpu.CompilerParams(dimension_semantics=sem, vmem_limit_bytes=VMEM_LIMIT)


def _dot(a, b):
    return jnp.dot(a, b, preferred_element_type=F32)


def _dot_nt(a, b):
    return lax.dot_general(a, b, (((1,), (1,)), ((), ())), preferred_element_type=F32)


def _dot_tn(a, b):
    return lax.dot_general(a, b, (((0,), (0,)), ((), ())), preferred_element_type=F32)


def _sigmoid(x):
    return 1.0 / (1.0 + jnp.exp(-x))


def _log_sigmoid(x):
    return jnp.minimum(x, 0.0) - jnp.log(1.0 + jnp.exp(-jnp.abs(x)))


def _inproj_body(x_ref, w_ref, c_ref, s1_ref, s2_ref, o_ref, *, tn):
    j = pl.program_id(1)
    acc = _dot(x_ref[...], w_ref[...])
    first_rot = COL_BQ // tn
    first_k = COL_BK // tn
    end_rot = COL_BV // tn
    is_rot = jnp.logical_and(j >= first_rot, j < end_rot)

    @pl.when(is_rot)
    def _():
        scale = jnp.where(j < first_k, DA_DH ** -0.5, 1.0).astype(F32)
        c = c_ref[...]
        s1 = s1_ref[...]
        s2 = s2_ref[...]
        for g in range(tn // LANES):
            t = acc[:, g * LANES:(g + 1) * LANES]
            r = (t * c + pltpu.roll(t, ROT_DIM // 2, 1) * s1
                 + pltpu.roll(t, LANES - ROT_DIM // 2, 1) * s2)
            o_ref[:, g * LANES:(g + 1) * LANES] = (r * scale).astype(o_ref.dtype)

    @pl.when(jnp.logical_not(is_rot))
    def _():
        o_ref[...] = acc.astype(o_ref.dtype)


def _inproj(h, w, rot_c, rot_s1, rot_s2, seq, *, tm=1024, tn=1024):
    m, k = h.shape
    n = w.shape[1]
    tm = min(tm, seq)
    nseq = seq // tm
    tab = pl.BlockSpec((tm, LANES), lambda i, j: (i % nseq, 0))
    return pl.pallas_call(
        functools.partial(_inproj_body, tn=tn),
        out_shape=jax.ShapeDtypeStruct((m, n), BF16),
        grid=(m // tm, n // tn),
        in_specs=[pl.BlockSpec((tm, k), lambda i, j: (i, 0)),
                  pl.BlockSpec((k, tn), lambda i, j: (0, j)),
                  tab, tab, tab],
        out_specs=pl.BlockSpec((tm, tn), lambda i, j: (i, j)),
        compiler_params=_params(("parallel", "arbitrary")),
        name="inproj",
    )(h, w, rot_c, rot_s1, rot_s2)


def _gate_body(x_ref, w_ref, b_ref, o_ref):
    acc = _dot(x_ref[...], w_ref[...]) + b_ref[...]
    o_ref[...] = _sigmoid(acc).astype(o_ref.dtype)


def _gate_proj(h, w, b, *, tm=1024, tn=1024):
    m, k = h.shape
    n = w.shape[1]
    tm = min(tm, m)
    return pl.pallas_call(
        _gate_body,
        out_shape=jax.ShapeDtypeStruct((m, n), BF16),
        grid=(m // tm, n // tn),
        in_specs=[pl.BlockSpec((tm, k), lambda i, j: (i, 0)),
                  pl.BlockSpec((k, tn), lambda i, j: (0, j)),
                  pl.BlockSpec((1, tn), lambda i, j: (0, j))],
        out_specs=pl.BlockSpec((tm, tn), lambda i, j: (i, j)),
        compiler_params=_params(("parallel", "arbitrary")),
        name="gate_proj",
    )(h, w, b)


def _up_body(x_ref, w_ref, o_ref):
    r = jnp.maximum(_dot(x_ref[...], w_ref[...]), 0.0)
    o_ref[...] = (r * r).astype(o_ref.dtype)


def _up_proj(h, w, *, tm=1024, tn=1024):
    m, k = h.shape
    n = w.shape[1]
    tm = min(tm, m)
    return pl.pallas_call(
        _up_body,
        out_shape=jax.ShapeDtypeStruct((m, n), BF16),
        grid=(m // tm, n // tn),
        in_specs=[pl.BlockSpec((tm, k), lambda i, j: (i, 0)),
                  pl.BlockSpec((k, tn), lambda i, j: (0, j))],
        out_specs=pl.BlockSpec((tm, tn), lambda i, j: (i, j)),
        compiler_params=_params(("parallel", "arbitrary")),
        name="up_proj",
    )(h, w)


def _mm_ln_body(y_ref, w_ref, res_ref, g_ref, b_ref, of_ref, ob_ref, acc_ref, *, nk):
    kk = pl.program_id(1)

    @pl.when(kk == 0)
    def _():
        acc_ref[...] = jnp.zeros_like(acc_ref)

    acc_ref[...] += _dot(y_ref[...], w_ref[...])

    @pl.when(kk == nk - 1)
    def _():
        g = g_ref[...]
        b = b_ref[...]

        def rows(i, carry):
            r = pl.ds(pl.multiple_of(i * LN_ROWS, LN_ROWS), LN_ROWS)
            t = ALPHA * res_ref[r, :] + acc_ref[r, :]
            mu = jnp.mean(t, axis=-1, keepdims=True)
            d = t - mu
            var = jnp.mean(d * d, axis=-1, keepdims=True)
            out = d * lax.rsqrt(var + EPS) * g + b
            of_ref[r, :] = out
            ob_ref[r, :] = out.astype(ob_ref.dtype)
            return carry

        lax.fori_loop(0, acc_ref.shape[0] // LN_ROWS, rows, 0)


def _mm_ln(y, w, res, g, b, *, tm=512, tk=1024, name):
    m, k = y.shape
    n = w.shape[1]
    tm = min(tm, m)
    nk = k // tk
    once = pl.Buffered(1)
    return pl.pallas_call(
        functools.partial(_mm_ln_body, nk=nk),
        out_shape=(jax.ShapeDtypeStruct((m, n), F32), jax.ShapeDtypeStruct((m, n), BF16)),
        grid=(m // tm, nk),
        in_specs=[pl.BlockSpec((tm, tk), lambda i, kk: (i, kk)),
                  pl.BlockSpec((tk, n), lambda i, kk: (kk, 0)),
                  pl.BlockSpec((tm, n), lambda i, kk: (i, 0), pipeline_mode=once),
                  pl.BlockSpec((1, n), lambda i, kk: (0, 0)),
                  pl.BlockSpec((1, n), lambda i, kk: (0, 0))],
        out_specs=(pl.BlockSpec((tm, n), lambda i, kk: (i, 0), pipeline_mode=once),
                   pl.BlockSpec((tm, n), lambda i, kk: (i, 0), pipeline_mode=once)),
        scratch_shapes=[pltpu.VMEM((tm, n), F32)],
        compiler_params=_params(("parallel", "arbitrary")),
        name=name,
    )(y, w, res, g, b)


def _merge_body(ya_ref, yb_ref, yc_ref, wa_ref, wb_ref, wc_ref, ga_ref, gb_ref, gc_ref, o_ref):
    acc = ga_ref[...].astype(F32) * _dot(ya_ref[...], wa_ref[...])
    acc += gb_ref[...].astype(F32) * _dot(yb_ref[...], wb_ref[...])
    acc += gc_ref[...].astype(F32) * _dot(yc_ref[...], wc_ref[...])
    o_ref[...] = acc.astype(o_ref.dtype)


def _merge(ya, yb, yc, wa, wb, wc, gates, *, tm=1024, tn=512):
    m = ya.shape[0]
    n = wa.shape[1]
    tm = min(tm, m)
    nb = n // tn

    def act(width):
        return pl.BlockSpec((tm, width), lambda i, j: (i, 0))

    def wt(width):
        return pl.BlockSpec((width, tn), lambda i, j: (0, j))

    def gate(branch):
        return pl.BlockSpec((tm, tn), lambda i, j: (i, branch * nb + j))

    return pl.pallas_call(
        _merge_body,
        out_shape=jax.ShapeDtypeStruct((m, n), BF16),
        grid=(m // tm, nb),
        in_specs=[act(ya.shape[1]), act(yb.shape[1]), act(yc.shape[1]),
                  wt(wa.shape[0]), wt(wb.shape[0]), wt(wc.shape[0]),
                  gate(0), gate(1), gate(2)],
        out_specs=pl.BlockSpec((tm, tn), lambda i, j: (i, j)),
        compiler_params=_params(("parallel", "arbitrary")),
        name="merge",
    )(ya, yb, yc, wa, wb, wc, gates, gates, gates)


def _attn_body(q_ref, k_ref, v_ref, lam_ref, g_ref, o_ref, m_ref, l_ref, acc_ref, *, tk, lam_init):
    seq = k_ref.shape[0]
    m_ref[...] = jnp.full_like(m_ref, -jnp.inf)
    l_ref[...] = jnp.zeros_like(l_ref)
    acc_ref[...] = jnp.zeros_like(acc_ref)

    def step(i, carry):
        start = pl.multiple_of(i * tk, tk)
        ks = k_ref[pl.ds(start, tk), :]
        vs = v_ref[pl.ds(start, tk), :]
        for c in range(2):
            s = _dot_nt(q_ref[:, c * DA_DH:(c + 1) * DA_DH], ks[:, c * DA_DH:(c + 1) * DA_DH])
            m_old = m_ref[c]
            m_new = jnp.maximum(m_old, jnp.max(s, axis=-1, keepdims=True))
            alpha = jnp.exp(m_old - m_new)
            p = jnp.exp(s - m_new)
            l_ref[c] = alpha * l_ref[c] + jnp.sum(p, axis=-1, keepdims=True)
            acc_ref[c] = alpha * acc_ref[c] + _dot(p.astype(BF16), vs)
            m_ref[c] = m_new
        return carry

    lax.fori_loop(0, seq // tk, step, 0)

    lp = lam_ref[...]
    lam = (jnp.exp(jnp.sum(lp[0:1] * lp[1:2], axis=-1, keepdims=True))
           - jnp.exp(jnp.sum(lp[2:3] * lp[3:4], axis=-1, keepdims=True)) + lam_init)
    o = acc_ref[0] / l_ref[0] - lam * (acc_ref[1] / l_ref[1])
    ms = jnp.mean(o * o, axis=-1, keepdims=True)
    o = o * lax.rsqrt(ms + EPS) * g_ref[...] * (1.0 - lam_init)
    o_ref[...] = o.astype(o_ref.dtype)


def _attention(proj3, lam_params, norm_g, layer, *, tq=1024, tk=1024):
    bsz, seq, _ = proj3.shape
    tq = min(tq, seq)
    tk = min(tk, seq)
    hw = 2 * DA_DH
    lam_init = 0.8 - 0.6 * math.exp(-0.3 * layer)
    return pl.pallas_call(
        functools.partial(_attn_body, tk=tk, lam_init=lam_init),
        out_shape=jax.ShapeDtypeStruct((bsz, seq, DA_V_WIDTH), BF16),
        grid=(bsz, DA_HEADS, seq // tq),
        in_specs=[pl.BlockSpec((None, tq, hw), lambda b, h, i: (b, i, COL_BQ // hw + h)),
                  pl.BlockSpec((None, seq, hw), lambda b, h, i: (b, 0, COL_BK // hw + h)),
                  pl.BlockSpec((None, seq, hw), lambda b, h, i: (b, 0, COL_BV // hw + h)),
                  pl.BlockSpec((4, DA_DH), lambda b, h, i: (0, 0)),
                  pl.BlockSpec((1, hw), lambda b, h, i: (0, 0))],
        out_specs=pl.BlockSpec((None, tq, hw), lambda b, h, i: (b, i, h)),
        scratch_shapes=[pltpu.VMEM((2, tq, 1), F32),
                        pltpu.VMEM((2, tq, 1), F32),
                        pltpu.VMEM((2, tq, hw), F32)],
        compiler_params=_params(("parallel", "parallel", "arbitrary")),
        name="diff_attention",
    )(proj3, proj3, proj3, lam_params, norm_g)


def _gla_gates(x, lb):
    ls = _log_sigmoid(x)
    if lb is None:
        return _sigmoid(-x), ls
    a = jnp.log(lb)
    b = jnp.log(1.0 - lb) + ls
    mx = jnp.maximum(a, b)
    log_f = mx + jnp.log(jnp.exp(a - mx) + jnp.exp(b - mx))
    return (1.0 - lb) * _sigmoid(-x), log_f


def _gla_chunk(q, x, v, state, lb, consts, reverse):
    tri, sel, blk_eq, row = consts
    cs, sub = GLA_CHUNK, GLA_SUB
    nsub = cs // sub
    k, g = _gla_gates(x, lb)
    b = jnp.dot(tri, g, precision=lax.Precision.HIGHEST, preferred_element_type=F32)
    edge = b[0:1] if reverse else b[cs - 1:cs]
    vb = v.astype(BF16)

    o = _dot_nt((q * jnp.exp(b)).astype(BF16), state.astype(BF16))
    k_edge = (k * jnp.exp(edge - b)).astype(BF16)
    new_state = jnp.exp(edge) * state + _dot_tn(vb, k_edge)

    qs, ks = [], []
    for j in range(nsub):
        if (reverse and j == 0) or (not reverse and j == nsub - 1):
            continue
        lo, hi = j * sub, (j + 1) * sub
        anchor = b[lo:lo + 1] if reverse else b[hi - 1:hi]
        q_ok = (row < lo) if reverse else (row >= hi)
        k_ok = jnp.logical_and(row >= lo, row < hi)
        qs.append(jnp.where(q_ok, q * jnp.exp(jnp.minimum(b - anchor, 0.0)), 0.0).astype(BF16))
        ks.append(jnp.where(k_ok, k * jnp.exp(jnp.minimum(anchor - b, 0.0)), 0.0).astype(BF16))
    a_mat = _dot_nt(jnp.concatenate(qs, axis=1), jnp.concatenate(ks, axis=1))

    sub_row = lax.broadcasted_iota(jnp.int32, (sub, 1), 0)
    cols = []
    for s in range(sub):
        keep = (sub_row <= s) if reverse else (sub_row >= s)
        parts = []
        for j in range(nsub):
            lo = j * sub
            bb = b[lo:lo + sub]
            w = jnp.exp(jnp.minimum(bb - b[lo + s:lo + s + 1], 0.0))
            parts.append(jnp.where(keep, q[lo:lo + sub] * k[lo + s:lo + s + 1] * w, 0.0).astype(BF16))
        cols.append(jnp.concatenate(parts, axis=0))
    diag = _dot(jnp.concatenate(cols, axis=1), sel)
    a_mat = a_mat + jnp.where(blk_eq, diag, 0.0)

    o = o + _dot(a_mat.astype(BF16), vb)
    return o, new_state


def _gla_body(q_ref, ff_ref, fb_ref, i_ref, g_ref, lbraw_ref, ng_ref, o_ref,
              of_ref, ob_ref, st_ref, *, layer):
    seq = q_ref.shape[0]
    cs, sub = GLA_CHUNK, GLA_SUB
    nc = seq // cs

    if layer == 0:
        lbs = (None, None)
    else:
        raw = lbraw_ref[...]
        e = jnp.exp(raw - jnp.max(raw, axis=0, keepdims=True))
        p = e / jnp.sum(e, axis=0, keepdims=True)
        lb2 = jnp.sum(p[1:layer + 1], axis=0)
        lbs = (lb2[0:1], lb2[1:2])

    r_i = lax.broadcasted_iota(jnp.int32, (cs, cs), 0)
    c_i = lax.broadcasted_iota(jnp.int32, (cs, cs), 1)
    tri_f = (c_i <= r_i).astype(F32)
    tri_b = (c_i >= r_i).astype(F32)
    blk_eq = (r_i // sub) == (c_i // sub)
    sr = lax.broadcasted_iota(jnp.int32, (sub * HG_DK, cs), 0)
    sc = lax.broadcasted_iota(jnp.int32, (sub * HG_DK, cs), 1)
    sel = ((sr // HG_DK) == (sc % sub)).astype(BF16)
    row = lax.broadcasted_iota(jnp.int32, (cs, 1), 0)

    st_ref[...] = jnp.zeros_like(st_ref)

    def step(i, carry):
        rf = pl.ds(pl.multiple_of(i * cs, cs), cs)
        rb = pl.ds(pl.multiple_of((nc - 1 - i) * cs, cs), cs)
        o_f, s_f = _gla_chunk(q_ref[rf, :].astype(F32), ff_ref[rf, :].astype(F32), i_ref[rf, :].astype(F32),
                              st_ref[0], lbs[0], (tri_f, sel, blk_eq, row), False)
        of_ref[rf, :] = o_f
        st_ref[0] = s_f
        o_b, s_b = _gla_chunk(q_ref[rb, :].astype(F32), fb_ref[rb, :].astype(F32), i_ref[rb, :].astype(F32),
                              st_ref[1], lbs[1], (tri_b, sel, blk_eq, row), True)
        ob_ref[rb, :] = o_b
        st_ref[1] = s_b
        return carry

    lax.fori_loop(0, nc, step, 0)

    tr = min(512, seq)
    ng = ng_ref[...]

    def fin(i, carry):
        r = pl.ds(pl.multiple_of(i * tr, tr), tr)
        o = of_ref[r, :] + ob_ref[r, :]
        ms = jnp.mean(o * o, axis=-1, keepdims=True)
        y = o * lax.rsqrt(ms + EPS) * ng
        gate = g_ref[r, :].astype(F32)
        o_ref[r, :] = (y * gate * _sigmoid(gate)).astype(o_ref.dtype)
        return carry

    lax.fori_loop(0, seq // tr, fin, 0)


def _hgrn2(proj3, lb_raw, norm_g, layer):
    bsz, seq, _ = proj3.shape

    def col(base):
        return pl.BlockSpec((None, seq, HG_DK), lambda b, h: (b, 0, base // HG_DK + h))

    return pl.pallas_call(
        functools.partial(_gla_body, layer=layer),
        out_shape=jax.ShapeDtypeStruct((bsz, seq, HG_WIDTH), BF16),
        grid=(bsz, HG_HEADS),
        in_specs=[col(COL_AQ), col(COL_AFF), col(COL_AFB), col(COL_AI), col(COL_AG),
                  pl.BlockSpec((DEPTH, 2, HG_DK), lambda b, h: (0, 0, h)),
                  pl.BlockSpec((1, HG_DK), lambda b, h: (0, h))],
        out_specs=pl.BlockSpec((None, seq, HG_DK), lambda b, h: (b, 0, h)),
        scratch_shapes=[pltpu.VMEM((seq, HG_DK), F32),
                        pltpu.VMEM((seq, HG_DK), F32),
                        pltpu.VMEM((2, HG_DK, HG_DK), F32)],
        compiler_params=_params(("parallel", "parallel")),
        name="hgrn2",
    )(proj3, proj3, proj3, proj3, proj3, lb_raw, norm_g)


def _gelu(x):
    return 0.5 * x * (1.0 + jnp.tanh(math.sqrt(2.0 / math.pi) * (x + 0.044715 * (x * x * x))))


def _sgu_body(u_ref, v_ref, ng_ref, nb_ref, ws_ref, bs_ref, o_ref):
    tm = u_ref.shape[0]
    v = _gelu(v_ref[...].astype(F32))
    mu = jnp.mean(v, axis=-1, keepdims=True)
    d = v - mu
    var = jnp.mean(d * d, axis=-1, keepdims=True)
    vn = (d * lax.rsqrt(var + EPS) * ng_ref[...] + nb_ref[...]).astype(BF16)
    bs = bs_ref[...]
    for c in range(tm // SG_CHUNK):
        r0 = c * SG_CHUNK
        for g in range(SG_GROUPS):
            c0 = g * SG_GDIM
            mixed = _dot(ws_ref[g], vn[r0:r0 + SG_CHUNK, c0:c0 + SG_GDIM]) + bs[:, g:g + 1]
            u = _gelu(u_ref[r0:r0 + SG_CHUNK, c0:c0 + SG_GDIM].astype(F32))
            o_ref[r0:r0 + SG_CHUNK, c0:c0 + SG_GDIM] = (u * mixed).astype(o_ref.dtype)


def _spatial_gating(proj, norm_g, norm_b, w_s, b_s_t, *, tm=256):
    m = proj.shape[0]
    return pl.pallas_call(
        _sgu_body,
        out_shape=jax.ShapeDtypeStruct((m, SG_WIDTH), BF16),
        grid=(m // tm,),
        in_specs=[pl.BlockSpec((tm, SG_WIDTH), lambda i: (i, COL_CU // SG_WIDTH)),
                  pl.BlockSpec((tm, SG_WIDTH), lambda i: (i, COL_CV // SG_WIDTH)),
                  pl.BlockSpec((1, SG_WIDTH), lambda i: (0, 0)),
                  pl.BlockSpec((1, SG_WIDTH), lambda i: (0, 0)),
                  pl.BlockSpec((SG_GROUPS, SG_CHUNK, SG_CHUNK), lambda i: (0, 0, 0)),
                  pl.BlockSpec((SG_CHUNK, SG_GROUPS), lambda i: (0, 0))],
        out_specs=pl.BlockSpec((tm, SG_WIDTH), lambda i: (i, 0)),
        compiler_params=_params(("parallel",)),
        name="spatial_gating",
    )(proj, proj, norm_g, norm_b, w_s, b_s_t)


def _rotary_tables(seq):
    half = ROT_DIM // 2
    pos = jnp.arange(seq, dtype=F32)
    freqs = ROPE_THETA ** (-jnp.arange(0, ROT_DIM, 2, dtype=F32) / ROT_DIM)
    ang = pos[:, None] * freqs[None, :]
    cos, sin = jnp.cos(ang), jnp.sin(ang)
    rest = LANES - ROT_DIM
    c = jnp.concatenate([cos, cos, jnp.ones((seq, rest), F32)], axis=1)
    s1 = jnp.concatenate([jnp.zeros((seq, half), F32), sin, jnp.zeros((seq, rest), F32)], axis=1)
    s2 = jnp.concatenate([-sin, jnp.zeros((seq, LANES - half), F32)], axis=1)
    return c, s1, s2


def kernel(x, w_in, hg_lb_raw, hg_norm_g, da_lambda, da_norm_g, sg_norm_g, sg_norm_b, sg_w_s, sg_b_s, w_branch_a, w_branch_b, w_branch_c, w_gate, b_gate, w_out, ln1_g, ln1_b, w_up, w_down, ln2_g, ln2_b):
    bsz, seq, d = x.shape
    m = bsz * seq
    rot_c, rot_s1, rot_s2 = _rotary_tables(seq)
    xf = x.reshape(m, d)
    xb = xf.astype(BF16)
    for layer in range(DEPTH):
        proj = _inproj(xb, w_in[layer].astype(BF16), rot_c, rot_s1, rot_s2, seq)
        gates = _gate_proj(xb, w_gate[layer].astype(BF16), b_gate[layer].reshape(1, -1))
        proj3 = proj.reshape(bsz, seq, D_IN)
        y_a = _hgrn2(proj3, hg_lb_raw, hg_norm_g[layer].reshape(1, -1), layer)
        y_b = _attention(proj3, da_lambda[layer], da_norm_g[layer].reshape(1, -1), layer)
        y_c = _spatial_gating(proj, sg_norm_g[layer].reshape(1, -1), sg_norm_b[layer].reshape(1, -1),
                              sg_w_s[layer].astype(BF16), sg_b_s[layer].T)
        merged = _merge(y_a.reshape(m, HG_WIDTH), y_b.reshape(m, DA_V_WIDTH), y_c,
                        w_branch_a[layer].astype(BF16), w_branch_b[layer].astype(BF16),
                        w_branch_c[layer].astype(BF16), gates)
        xf, xb = _mm_ln(merged, w_out[layer].astype(BF16), xf, ln1_g[layer].reshape(1, -1),
                        ln1_b[layer].reshape(1, -1), name="out_proj_ln")
        hid = _up_proj(xb, w_up[layer].astype(BF16))
        xf, xb = _mm_ln(hid, w_down[layer].astype(BF16), xf, ln2_g[layer].reshape(1, -1),
                        ln2_b[layer].reshape(1, -1), name="down_proj_ln")
    return xf.reshape(bsz, seq, d)
```

```python
import functools
import math

import jax
import jax.numpy as jnp
from jax import lax
from jax.experimental import pallas as pl
from jax.experimental.pallas import tpu as pltpu

D_MODEL = 4096
DEPTH = 2
HG_WIDTH = D_MODEL // 4
HG_DK = 128
HG_HEADS = HG_WIDTH // HG_DK
DA_DH = 128
DA_HEADS = D_MODEL // (4 * DA_DH)
DA_QK_WIDTH = DA_HEADS * 2 * DA_DH
DA_V_WIDTH = DA_HEADS * 2 * DA_DH
ROPE_THETA = 500000.0
ROT_DIM = DA_DH // 4
SG_CHUNK = 128
SG_WIDTH = D_MODEL // 4
SG_GDIM = 128
SG_GROUPS = SG_WIDTH // SG_GDIM
N_BRANCH = 3
D_FF = 4 * D_MODEL
ALPHA = (2.0 * DEPTH) ** 0.25
EPS = 1e-5
D_IN = 5 * HG_WIDTH + 2 * DA_QK_WIDTH + DA_V_WIDTH + 2 * SG_WIDTH

COL_AQ, COL_AFF, COL_AFB, COL_AI, COL_AG = (i * HG_WIDTH for i in range(5))
COL_BQ = 5 * HG_WIDTH
COL_BK = COL_BQ + DA_QK_WIDTH
COL_BV = COL_BK + DA_QK_WIDTH
COL_CU = COL_BV + DA_V_WIDTH
COL_CV = COL_CU + SG_WIDTH

LANES = 128
VMEM_LIMIT = 56 * 1024 * 1024
GLA_CHUNK = 64
GLA_SUB = 16
LN_ROWS = 64
ATT_STRIP = 256
GLA_UNROLL = 4
LOG2E = math.log2(math.e)

F32 = jnp.float32
BF16 = jnp.bfloat16


def _params(sem):
    return pltpu.CompilerParams(dimension_semantics=sem, vmem_limit_bytes=VMEM_LIMIT)


def _dot(a, b):
    return jnp.dot(a, b, preferred_element_type=F32)


def _dot_nt(a, b):
    return lax.dot_general(a, b, (((1,), (1,)), ((), ())), preferred_element_type=F32)


def _dot_tn(a, b):
    return lax.dot_general(a, b, (((0,), (0,)), ((), ())), preferred_element_type=F32)


def _sigmoid(x):
    return 1.0 / (1.0 + jnp.exp(-x))


def _log_sigmoid(x):
    return jnp.minimum(x, 0.0) - jnp.log(1.0 + jnp.exp(-jnp.abs(x)))


def _inproj_body(x_ref, w_ref, c_ref, s1_ref, s2_ref, o_ref, *, tn):
    j = pl.program_id(1)
    acc = _dot(x_ref[...], w_ref[...])
    first_rot = COL_BQ // tn
    first_k = COL_BK // tn
    end_rot = COL_BV // tn
    is_rot = jnp.logical_and(j >= first_rot, j < end_rot)

    @pl.when(is_rot)
    def _():
        scale = jnp.where(j < first_k, LOG2E * DA_DH ** -0.5, 1.0).astype(F32)
        c = c_ref[...]
        s1 = s1_ref[...]
        s2 = s2_ref[...]
        for g in range(tn // LANES):
            t = acc[:, g * LANES:(g + 1) * LANES]
            r = (t * c + pltpu.roll(t, ROT_DIM // 2, 1) * s1
                 + pltpu.roll(t, LANES - ROT_DIM // 2, 1) * s2)
            o_ref[:, g * LANES:(g + 1) * LANES] = (r * scale).astype(o_ref.dtype)

    @pl.when(jnp.logical_not(is_rot))
    def _():
        o_ref[...] = acc.astype(o_ref.dtype)


def _inproj(h, w, rot_c, rot_s1, rot_s2, seq, *, tm=1024, tn=1024):
    m, k = h.shape
    n = w.shape[1]
    tm = min(tm, seq)
    nseq = seq // tm
    tab = pl.BlockSpec((tm, LANES), lambda i, j: (i % nseq, 0))
    return pl.pallas_call(
        functools.partial(_inproj_body, tn=tn),
        out_shape=jax.ShapeDtypeStruct((m, n), BF16),
        grid=(m // tm, n // tn),
        in_specs=[pl.BlockSpec((tm, k), lambda i, j: (i, 0)),
                  pl.BlockSpec((k, tn), lambda i, j: (0, j)),
                  tab, tab, tab],
        out_specs=pl.BlockSpec((tm, tn), lambda i, j: (i, j)),
        compiler_params=_params(("parallel", "arbitrary")),
        name="inproj",
    )(h, w, rot_c, rot_s1, rot_s2)


def _gate_body(x_ref, w_ref, b_ref, o_ref):
    acc = _dot(x_ref[...], w_ref[...]) + b_ref[...]
    o_ref[...] = _sigmoid(acc).astype(o_ref.dtype)


def _gate_proj(h, w, b, *, tm=1024, tn=1024):
    m, k = h.shape
    n = w.shape[1]
    tm = min(tm, m)
    return pl.pallas_call(
        _gate_body,
        out_shape=jax.ShapeDtypeStruct((m, n), BF16),
        grid=(m // tm, n // tn),
        in_specs=[pl.BlockSpec((tm, k), lambda i, j: (i, 0)),
                  pl.BlockSpec((k, tn), lambda i, j: (0, j)),
                  pl.BlockSpec((1, tn), lambda i, j: (0, j))],
        out_specs=pl.BlockSpec((tm, tn), lambda i, j: (i, j)),
        compiler_params=_params(("parallel", "arbitrary")),
        name="gate_proj",
    )(h, w, b)


def _up_body(x_ref, w_ref, o_ref):
    r = jnp.maximum(_dot(x_ref[...], w_ref[...]), 0.0)
    o_ref[...] = (r * r).astype(o_ref.dtype)


def _up_proj(h, w, *, tm=1024, tn=1024):
    m, k = h.shape
    n = w.shape[1]
    tm = min(tm, m)
    return pl.pallas_call(
        _up_body,
        out_shape=jax.ShapeDtypeStruct((m, n), BF16),
        grid=(m // tm, n // tn),
        in_specs=[pl.BlockSpec((tm, k), lambda i, j: (i, 0)),
                  pl.BlockSpec((k, tn), lambda i, j: (0, j))],
        out_specs=pl.BlockSpec((tm, tn), lambda i, j: (i, j)),
        compiler_params=_params(("parallel", "arbitrary")),
        name="up_proj",
    )(h, w)


def _mm_ln_body(y_ref, w_ref, res_ref, g_ref, b_ref, of_ref, ob_ref, acc_ref, *, nk):
    kk = pl.program_id(1)

    @pl.when(kk == 0)
    def _():
        acc_ref[...] = jnp.zeros_like(acc_ref)

    acc_ref[...] += _dot(y_ref[...], w_ref[...])

    @pl.when(kk == nk - 1)
    def _():
        g = g_ref[...]
        b = b_ref[...]

        def rows(i, carry):
            r = pl.ds(pl.multiple_of(i * LN_ROWS, LN_ROWS), LN_ROWS)
            t = ALPHA * res_ref[r, :] + acc_ref[r, :]
            mu = jnp.mean(t, axis=-1, keepdims=True)
            d = t - mu
            var = jnp.mean(d * d, axis=-1, keepdims=True)
            out = d * lax.rsqrt(var + EPS) * g + b
            of_ref[r, :] = out
            ob_ref[r, :] = out.astype(ob_ref.dtype)
            return carry

        lax.fori_loop(0, acc_ref.shape[0] // LN_ROWS, rows, 0)


def _mm_ln(y, w, res, g, b, *, tm=512, tk=1024, name):
    m, k = y.shape
    n = w.shape[1]
    tm = min(tm, m)
    nk = k // tk
    once = pl.Buffered(1)
    return pl.pallas_call(
        functools.partial(_mm_ln_body, nk=nk),
        out_shape=(jax.ShapeDtypeStruct((m, n), F32), jax.ShapeDtypeStruct((m, n), BF16)),
        grid=(m // tm, nk),
        in_specs=[pl.BlockSpec((tm, tk), lambda i, kk: (i, kk)),
                  pl.BlockSpec((tk, n), lambda i, kk: (kk, 0)),
                  pl.BlockSpec((tm, n), lambda i, kk: (i, 0), pipeline_mode=once),
                  pl.BlockSpec((1, n), lambda i, kk: (0, 0)),
                  pl.BlockSpec((1, n), lambda i, kk: (0, 0))],
        out_specs=(pl.BlockSpec((tm, n), lambda i, kk: (i, 0), pipeline_mode=once),
                   pl.BlockSpec((tm, n), lambda i, kk: (i, 0), pipeline_mode=once)),
        scratch_shapes=[pltpu.VMEM((tm, n), F32)],
        compiler_params=_params(("parallel", "arbitrary")),
        name=name,
    )(y, w, res, g, b)


def _merge_body(ya_ref, yb_ref, yc_ref, wa_ref, wb_ref, wc_ref, ga_ref, gb_ref, gc_ref, o_ref):
    acc = ga_ref[...].astype(F32) * _dot(ya_ref[...], wa_ref[...])
    acc += gb_ref[...].astype(F32) * _dot(yb_ref[...], wb_ref[...])
    acc += gc_ref[...].astype(F32) * _dot(yc_ref[...], wc_ref[...])
    o_ref[...] = acc.astype(o_ref.dtype)


def _merge(ya, yb, yc, wa, wb, wc, gates, *, tm=1024, tn=512):
    m = ya.shape[0]
    n = wa.shape[1]
    tm = min(tm, m)
    nb = n // tn

    def act(width):
        return pl.BlockSpec((tm, width), lambda i, j: (i, 0))

    def wt(width):
        return pl.BlockSpec((width, tn), lambda i, j: (0, j))

    def gate(branch):
        return pl.BlockSpec((tm, tn), lambda i, j: (i, branch * nb + j))

    return pl.pallas_call(
        _merge_body,
        out_shape=jax.ShapeDtypeStruct((m, n), BF16),
        grid=(m // tm, nb),
        in_specs=[act(ya.shape[1]), act(yb.shape[1]), act(yc.shape[1]),
                  wt(wa.shape[0]), wt(wb.shape[0]), wt(wc.shape[0]),
                  gate(0), gate(1), gate(2)],
        out_specs=pl.BlockSpec((tm, tn), lambda i, j: (i, j)),
        compiler_params=_params(("parallel", "arbitrary")),
        name="merge",
    )(ya, yb, yc, wa, wb, wc, gates, gates, gates)


def _attn_body(q_ref, k_ref, v_ref, lam_ref, g_ref, o_ref, qt_ref, vt_ref, m_ref, l_ref, acc_ref,
               *, tk, lam_init):
    seq = k_ref.shape[0]
    tq = q_ref.shape[0]
    nkv = seq // tk

    @pl.when(pl.program_id(2) == 0)
    def _():
        def transpose_values(i, carry):
            r = pl.ds(pl.multiple_of(i * tk, tk), tk)
            vt_ref[i] = v_ref[r, :].astype(F32).T.astype(BF16)
            return carry

        lax.fori_loop(0, nkv, transpose_values, 0)

    qt_ref[...] = q_ref[...].astype(F32).T.astype(BF16)
    m_ref[...] = jnp.full_like(m_ref, -jnp.inf)
    l_ref[...] = jnp.zeros_like(l_ref)
    acc_ref[...] = jnp.zeros_like(acc_ref)

    def step(i, carry):
        rows = pl.ds(pl.multiple_of(i * tk, tk), tk)
        vt = vt_ref[i]
        for c in range(2):
            kc = k_ref[rows, c * DA_DH:(c + 1) * DA_DH]
            for j in range(tq // ATT_STRIP):
                cols = slice(j * ATT_STRIP, (j + 1) * ATT_STRIP)
                st = _dot(kc, qt_ref[c * DA_DH:(c + 1) * DA_DH, cols])
                m_old = m_ref[c, :, cols]
                m_new = jnp.maximum(m_old, jnp.max(st, axis=0, keepdims=True))
                alpha = jnp.exp2(m_old - m_new)
                p = jnp.exp2(st - m_new)
                l_ref[c, :, cols] = alpha * l_ref[c, :, cols] + jnp.sum(p, axis=0, keepdims=True)
                acc_ref[c, :, cols] = alpha * acc_ref[c, :, cols] + _dot(vt, p.astype(BF16))
                m_ref[c, :, cols] = m_new
        return carry

    lax.fori_loop(0, nkv, step, 0, unroll=8)

    lp = lam_ref[...]
    lam = (jnp.exp(jnp.sum(lp[0:1] * lp[1:2], axis=-1, keepdims=True))
           - jnp.exp(jnp.sum(lp[2:3] * lp[3:4], axis=-1, keepdims=True)) + lam_init)
    o = acc_ref[0] / l_ref[0] - lam * (acc_ref[1] / l_ref[1])
    ms = jnp.mean(o * o, axis=0, keepdims=True)
    o = o * lax.rsqrt(ms + EPS) * g_ref[...] * (1.0 - lam_init)
    o_ref[...] = o.T.astype(o_ref.dtype)


def _attention(proj3, lam_params, norm_g_col, layer, *, tq=1024, tk=256):
    bsz, seq, _ = proj3.shape
    tq = min(tq, seq)
    tk = min(tk, seq)
    hw = 2 * DA_DH
    lam_init = 0.8 - 0.6 * math.exp(-0.3 * layer)
    return pl.pallas_call(
        functools.partial(_attn_body, tk=tk, lam_init=lam_init),
        out_shape=jax.ShapeDtypeStruct((bsz, seq, DA_V_WIDTH), BF16),
        grid=(bsz, DA_HEADS, seq // tq),
        in_specs=[pl.BlockSpec((None, tq, hw), lambda b, h, i: (b, i, COL_BQ // hw + h)),
                  pl.BlockSpec((None, seq, hw), lambda b, h, i: (b, 0, COL_BK // hw + h)),
                  pl.BlockSpec((None, seq, hw), lambda b, h, i: (b, 0, COL_BV // hw + h)),
                  pl.BlockSpec((4, DA_DH), lambda b, h, i: (0, 0)),
                  pl.BlockSpec((hw, 1), lambda b, h, i: (0, 0))],
        out_specs=pl.BlockSpec((None, tq, hw), lambda b, h, i: (b, i, h)),
        scratch_shapes=[pltpu.VMEM((hw, tq), BF16),
                        pltpu.VMEM((seq // tk, hw, tk), BF16),
                        pltpu.VMEM((2, 1, tq), F32),
                        pltpu.VMEM((2, 1, tq), F32),
                        pltpu.VMEM((2, hw, tq), F32)],
        compiler_params=_params(("parallel", "parallel", "arbitrary")),
        name="diff_attention",
    )(proj3, proj3, proj3, lam_params, norm_g_col)


def _gla_gates(x, lb):
    ls = _log_sigmoid(x)
    if lb is None:
        return _sigmoid(-x), ls
    a = jnp.log(lb)
    b = jnp.log(1.0 - lb) + ls
    mx = jnp.maximum(a, b)
    log_f = mx + jnp.log(jnp.exp(a - mx) + jnp.exp(b - mx))
    return (1.0 - lb) * _sigmoid(-x), log_f


def _gla_chunk(q, x, v, state, lb, consts, reverse):
    tri, sel, blk_eq, row = consts
    cs, sub = GLA_CHUNK, GLA_SUB
    nsub = cs // sub
    k, g = _gla_gates(x, lb)
    b = jnp.dot(tri, g, precision=lax.Precision.HIGHEST, preferred_element_type=F32)
    edge = b[0:1] if reverse else b[cs - 1:cs]
    vb = v.astype(BF16)

    o = _dot_nt((q * jnp.exp(b)).astype(BF16), state.astype(BF16))
    k_edge = (k * jnp.exp(edge - b)).astype(BF16)
    new_state = jnp.exp(edge) * state + _dot_tn(vb, k_edge)

    qs, ks = [], []
    for j in range(nsub):
        if (reverse and j == 0) or (not reverse and j == nsub - 1):
            continue
        lo, hi = j * sub, (j + 1) * sub
        anchor = b[lo:lo + 1] if reverse else b[hi - 1:hi]
        q_ok = (row < lo) if reverse else (row >= hi)
        k_ok = jnp.logical_and(row >= lo, row < hi)
        qs.append(jnp.where(q_ok, q * jnp.exp(jnp.minimum(b - anchor, 0.0)), 0.0).astype(BF16))
        ks.append(jnp.where(k_ok, k * jnp.exp(jnp.minimum(anchor - b, 0.0)), 0.0).astype(BF16))
    a_mat = _dot_nt(jnp.concatenate(qs, axis=1), jnp.concatenate(ks, axis=1))

    sub_row = lax.broadcasted_iota(jnp.int32, (sub, 1), 0)
    cols = []
    for s in range(sub):
        keep = (sub_row <= s) if reverse else (sub_row >= s)
        parts = []
        for j in range(nsub):
            lo = j * sub
            bb = b[lo:lo + sub]
            w = jnp.exp(jnp.minimum(bb - b[lo + s:lo + s + 1], 0.0))
            parts.append(jnp.where(keep, q[lo:lo + sub] * k[lo + s:lo + s + 1] * w, 0.0).astype(BF16))
        cols.append(jnp.concatenate(parts, axis=0))
    diag = _dot(jnp.concatenate(cols, axis=1), sel)
    a_mat = a_mat + jnp.where(blk_eq, diag, 0.0)

    o = o + _dot(a_mat.astype(BF16), vb)
    return o, new_state


def _gla_body(q_ref, ff_ref, fb_ref, i_ref, g_ref, lbraw_ref, ng_ref, o_ref,
              of_ref, ob_ref, st_ref, *, layer):
    seq = q_ref.shape[0]
    cs, sub = GLA_CHUNK, GLA_SUB
    nc = seq // cs

    if layer == 0:
        lbs = (None, None)
    else:
        raw = lbraw_ref[...]
        e = jnp.exp(raw - jnp.max(raw, axis=0, keepdims=True))
        p = e / jnp.sum(e, axis=0, keepdims=True)
        lb2 = jnp.sum(p[1:layer + 1], axis=0)
        lbs = (lb2[0:1], lb2[1:2])

    r_i = lax.broadcasted_iota(jnp.int32, (cs, cs), 0)
    c_i = lax.broadcasted_iota(jnp.int32, (cs, cs), 1)
    tri_f = (c_i <= r_i).astype(F32)
    tri_b = (c_i >= r_i).astype(F32)
    blk_eq = (r_i // sub) == (c_i // sub)
    sr = lax.broadcasted_iota(jnp.int32, (sub * HG_DK, cs), 0)
    sc = lax.broadcasted_iota(jnp.int32, (sub * HG_DK, cs), 1)
    sel = ((sr // HG_DK) == (sc % sub)).astype(BF16)
    row = lax.broadcasted_iota(jnp.int32, (cs, 1), 0)

    st_ref[...] = jnp.zeros_like(st_ref)

    def step(i, carry):
        rf = pl.ds(pl.multiple_of(i * cs, cs), cs)
        rb = pl.ds(pl.multiple_of((nc - 1 - i) * cs, cs), cs)
        o_f, s_f = _gla_chunk(q_ref[rf, :].astype(F32), ff_ref[rf, :].astype(F32), i_ref[rf, :].astype(F32),
                              st_ref[0], lbs[0], (tri_f, sel, blk_eq, row), False)
        of_ref[rf, :] = o_f
        st_ref[0] = s_f
        o_b, s_b = _gla_chunk(q_ref[rb, :].astype(F32), fb_ref[rb, :].astype(F32), i_ref[rb, :].astype(F32),
                              st_ref[1], lbs[1], (tri_b, sel, blk_eq, row), True)
        ob_ref[rb, :] = o_b
        st_ref[1] = s_b
        return carry

    lax.fori_loop(0, nc, step, 0, unroll=GLA_UNROLL)

    tr = min(512, seq)
    ng = ng_ref[...]

    def fin(i, carry):
        r = pl.ds(pl.multiple_of(i * tr, tr), tr)
        o = of_ref[r, :] + ob_ref[r, :]
        ms = jnp.mean(o * o, axis=-1, keepdims=True)
        y = o * lax.rsqrt(ms + EPS) * ng
        gate = g_ref[r, :].astype(F32)
        o_ref[r, :] = (y * gate * _sigmoid(gate)).astype(o_ref.dtype)
        return carry

    lax.fori_loop(0, seq // tr, fin, 0)


def _hgrn2(proj3, lb_raw, norm_g, layer):
    bsz, seq, _ = proj3.shape

    def col(base):
        return pl.BlockSpec((None, seq, HG_DK), lambda b, h: (b, 0, base // HG_DK + h))

    return pl.pallas_call(
        functools.partial(_gla_body, layer=layer),
        out_shape=jax.ShapeDtypeStruct((bsz, seq, HG_WIDTH), BF16),
        grid=(bsz, HG_HEADS),
        in_specs=[col(COL_AQ), col(COL_AFF), col(COL_AFB), col(COL_AI), col(COL_AG),
                  pl.BlockSpec((DEPTH, 2, HG_DK), lambda b, h: (0, 0, h)),
                  pl.BlockSpec((1, HG_DK), lambda b, h: (0, h))],
        out_specs=pl.BlockSpec((None, seq, HG_DK), lambda b, h: (b, 0, h)),
        scratch_shapes=[pltpu.VMEM((seq, HG_DK), F32),
                        pltpu.VMEM((seq, HG_DK), F32),
                        pltpu.VMEM((2, HG_DK, HG_DK), F32)],
        compiler_params=_params(("parallel", "parallel")),
        name="hgrn2",
    )(proj3, proj3, proj3, proj3, proj3, lb_raw, norm_g)


def _gelu(x):
    return 0.5 * x * (1.0 + jnp.tanh(math.sqrt(2.0 / math.pi) * (x + 0.044715 * (x * x * x))))


def _sgu_body(u_ref, v_ref, ng_ref, nb_ref, ws_ref, bs_ref, o_ref):
    tm = u_ref.shape[0]
    v = _gelu(v_ref[...].astype(F32))
    mu = jnp.mean(v, axis=-1, keepdims=True)
    d = v - mu
    var = jnp.mean(d * d, axis=-1, keepdims=True)
    vn = (d * lax.rsqrt(var + EPS) * ng_ref[...] + nb_ref[...]).astype(BF16)
    bs = bs_ref[...]
    for c in range(tm // SG_CHUNK):
        r0 = c * SG_CHUNK
        for g in range(SG_GROUPS):
            c0 = g * SG_GDIM
            mixed = _dot(ws_ref[g], vn[r0:r0 + SG_CHUNK, c0:c0 + SG_GDIM]) + bs[:, g:g + 1]
            u = _gelu(u_ref[r0:r0 + SG_CHUNK, c0:c0 + SG_GDIM].astype(F32))
            o_ref[r0:r0 + SG_CHUNK, c0:c0 + SG_GDIM] = (u * mixed).astype(o_ref.dtype)


def _spatial_gating(proj, norm_g, norm_b, w_s, b_s_t, *, tm=256):
    m = proj.shape[0]
    return pl.pallas_call(
        _sgu_body,
        out_shape=jax.ShapeDtypeStruct((m, SG_WIDTH), BF16),
        grid=(m // tm,),
        in_specs=[pl.BlockSpec((tm, SG_WIDTH), lambda i: (i, COL_CU // SG_WIDTH)),
                  pl.BlockSpec((tm, SG_WIDTH), lambda i: (i, COL_CV // SG_WIDTH)),
                  pl.BlockSpec((1, SG_WIDTH), lambda i: (0, 0)),
                  pl.BlockSpec((1, SG_WIDTH), lambda i: (0, 0)),
                  pl.BlockSpec((SG_GROUPS, SG_CHUNK, SG_CHUNK), lambda i: (0, 0, 0)),
                  pl.BlockSpec((SG_CHUNK, SG_GROUPS), lambda i: (0, 0))],
        out_specs=pl.BlockSpec((tm, SG_WIDTH), lambda i: (i, 0)),
        compiler_params=_params(("parallel",)),
        name="spatial_gating",
    )(proj, proj, norm_g, norm_b, w_s, b_s_t)


def _rotary_tables(seq):
    half = ROT_DIM // 2
    pos = jnp.arange(seq, dtype=F32)
    freqs = ROPE_THETA ** (-jnp.arange(0, ROT_DIM, 2, dtype=F32) / ROT_DIM)
    ang = pos[:, None] * freqs[None, :]
    cos, sin = jnp.cos(ang), jnp.sin(ang)
    rest = LANES - ROT_DIM
    c = jnp.concatenate([cos, cos, jnp.ones((seq, rest), F32)], axis=1)
    s1 = jnp.concatenate([jnp.zeros((seq, half), F32), sin, jnp.zeros((seq, rest), F32)], axis=1)
    s2 = jnp.concatenate([-sin, jnp.zeros((seq, LANES - half), F32)], axis=1)
    return c, s1, s2


def kernel(x, w_in, hg_lb_raw, hg_norm_g, da_lambda, da_norm_g, sg_norm_g, sg_norm_b, sg_w_s, sg_b_s, w_branch_a, w_branch_b, w_branch_c, w_gate, b_gate, w_out, ln1_g, ln1_b, w_up, w_down, ln2_g, ln2_b):
    bsz, seq, d = x.shape
    m = bsz * seq
    rot_c, rot_s1, rot_s2 = _rotary_tables(seq)
    xf = x.reshape(m, d)
    xb = xf.astype(BF16)
    for layer in range(DEPTH):
        proj = _inproj(xb, w_in[layer].astype(BF16), rot_c, rot_s1, rot_s2, seq)
        gates = _gate_proj(xb, w_gate[layer].astype(BF16), b_gate[layer].reshape(1, -1))
        proj3 = proj.reshape(bsz, seq, D_IN)
        y_a = _hgrn2(proj3, hg_lb_raw, hg_norm_g[layer].reshape(1, -1), layer)
        y_b = _attention(proj3, da_lambda[layer], da_norm_g[layer].reshape(-1, 1), layer)
        y_c = _spatial_gating(proj, sg_norm_g[layer].reshape(1, -1), sg_norm_b[layer].reshape(1, -1),
                              sg_w_s[layer].astype(BF16), sg_b_s[layer].T)
        merged = _merge(y_a.reshape(m, HG_WIDTH), y_b.reshape(m, DA_V_WIDTH), y_c,
                        w_branch_a[layer].astype(BF16), w_branch_b[layer].astype(BF16),
                        w_branch_c[layer].astype(BF16), gates)
        xf, xb = _mm_ln(merged, w_out[layer].astype(BF16), xf, ln1_g[layer].reshape(1, -1),
                        ln1_b[layer].reshape(1, -1), name="out_proj_ln")
        hid = _up_proj(xb, w_up[layer].astype(BF16))
        xf, xb = _mm_ln(hid, w_down[layer].astype(BF16), xf, ln2_g[layer].reshape(1, -1),
                        ln2_b[layer].reshape(1, -1), name="down_proj_ln")
    return xf.reshape(bsz, seq, d)
```

```python
import functools
import math

import jax
import jax.numpy as jnp
from jax import lax
from jax.experimental import pallas as pl
from jax.experimental.pallas import tpu as pltpu

D_MODEL = 4096
DEPTH = 2
HG_WIDTH = D_MODEL // 4
HG_DK = 128
HG_HEADS = HG_WIDTH // HG_DK
DA_DH = 128
DA_HEADS = D_MODEL // (4 * DA_DH)
DA_QK_WIDTH = DA_HEADS * 2 * DA_DH
DA_V_WIDTH = DA_HEADS * 2 * DA_DH
ROPE_THETA = 500000.0
ROT_DIM = DA_DH // 4
SG_CHUNK = 128
SG_WIDTH = D_MODEL // 4
SG_GDIM = 128
SG_GROUPS = SG_WIDTH // SG_GDIM
N_BRANCH = 3
D_FF = 4 * D_MODEL
ALPHA = (2.0 * DEPTH) ** 0.25
EPS = 1e-5
D_IN = 5 * HG_WIDTH + 2 * DA_QK_WIDTH + DA_V_WIDTH + 2 * SG_WIDTH

COL_AQ, COL_AFF, COL_AFB, COL_AI, COL_AG = (i * HG_WIDTH for i in range(5))
COL_BQ = 5 * HG_WIDTH
COL_BK = COL_BQ + DA_QK_WIDTH
COL_BV = COL_BK + DA_QK_WIDTH
COL_CU = COL_BV + DA_V_WIDTH
COL_CV = COL_CU + SG_WIDTH

LANES = 128
VMEM_LIMIT = 56 * 1024 * 1024
GLA_CHUNK = 64
GLA_SUB = 16
LN_ROWS = 64
ATT_STRIP = 256
GLA_UNROLL = 4
LOG2E = math.log2(math.e)

F32 = jnp.float32
BF16 = jnp.bfloat16


def _params(sem):
    return pltpu.CompilerParams(dimension_semantics=sem, vmem_limit_bytes=VMEM_LIMIT)


def _dot(a, b):
    return jnp.dot(a, b, preferred_element_type=F32)


def _dot_nt(a, b):
    return lax.dot_general(a, b, (((1,), (1,)), ((), ())), preferred_element_type=F32)


def _dot_tn(a, b):
    return lax.dot_general(a, b, (((0,), (0,)), ((), ())), preferred_element_type=F32)


def _sigmoid(x):
    return 1.0 / (1.0 + jnp.exp(-x))


def _inproj_body(x_ref, w_ref, c_ref, s1_ref, s2_ref, o_ref, *, tn):
    j = pl.program_id(1)
    acc = _dot(x_ref[...], w_ref[...].astype(BF16))
    first_rot = COL_BQ // tn
    first_k = COL_BK // tn
    end_rot = COL_BV // tn
    is_rot = jnp.logical_and(j >= first_rot, j < end_rot)

    @pl.when(is_rot)
    def _():
        scale = jnp.where(j < first_k, LOG2E * DA_DH ** -0.5, 1.0).astype(F32)
        c = c_ref[...]
        s1 = s1_ref[...]
        s2 = s2_ref[...]
        for g in range(tn // LANES):
            t = acc[:, g * LANES:(g + 1) * LANES]
            r = (t * c + pltpu.roll(t, ROT_DIM // 2, 1) * s1
                 + pltpu.roll(t, LANES - ROT_DIM // 2, 1) * s2)
            o_ref[:, g * LANES:(g + 1) * LANES] = (r * scale).astype(o_ref.dtype)

    @pl.when(jnp.logical_not(is_rot))
    def _():
        o_ref[...] = acc.astype(o_ref.dtype)


def _inproj(h, w, layer, rot_c, rot_s1, rot_s2, seq, *, tm=1024, tn=512):
    m, k = h.shape
    n = w.shape[2]
    tm = min(tm, seq)
    nseq = seq // tm
    tab = pl.BlockSpec((tm, LANES), lambda i, j: (i % nseq, 0))
    return pl.pallas_call(
        functools.partial(_inproj_body, tn=tn),
        out_shape=jax.ShapeDtypeStruct((m, n), BF16),
        grid=(m // tm, n // tn),
        in_specs=[pl.BlockSpec((tm, k), lambda i, j: (i, 0)),
                  pl.BlockSpec((None, k, tn), lambda i, j: (layer, 0, j)),
                  tab, tab, tab],
        out_specs=pl.BlockSpec((tm, tn), lambda i, j: (i, j)),
        compiler_params=_params(("parallel", "arbitrary")),
        name="inproj",
    )(h, w, rot_c, rot_s1, rot_s2)


def _gate_body(x_ref, w_ref, b_ref, o_ref):
    acc = _dot(x_ref[...], w_ref[...].astype(BF16)) + b_ref[...]
    o_ref[...] = _sigmoid(acc).astype(o_ref.dtype)


def _gate_proj(h, w, layer, b, *, tm=1024, tn=512):
    m, k = h.shape
    n = w.shape[2]
    tm = min(tm, m)
    return pl.pallas_call(
        _gate_body,
        out_shape=jax.ShapeDtypeStruct((m, n), BF16),
        grid=(m // tm, n // tn),
        in_specs=[pl.BlockSpec((tm, k), lambda i, j: (i, 0)),
                  pl.BlockSpec((None, k, tn), lambda i, j: (layer, 0, j)),
                  pl.BlockSpec((1, tn), lambda i, j: (0, j))],
        out_specs=pl.BlockSpec((tm, tn), lambda i, j: (i, j)),
        compiler_params=_params(("parallel", "arbitrary")),
        name="gate_proj",
    )(h, w, b)


def _up_body(x_ref, w_ref, o_ref):
    r = jnp.maximum(_dot(x_ref[...], w_ref[...].astype(BF16)), 0.0)
    o_ref[...] = (r * r).astype(o_ref.dtype)


def _up_proj(h, w, layer, *, tm=1024, tn=512):
    m, k = h.shape
    n = w.shape[2]
    tm = min(tm, m)
    return pl.pallas_call(
        _up_body,
        out_shape=jax.ShapeDtypeStruct((m, n), BF16),
        grid=(m // tm, n // tn),
        in_specs=[pl.BlockSpec((tm, k), lambda i, j: (i, 0)),
                  pl.BlockSpec((None, k, tn), lambda i, j: (layer, 0, j))],
        out_specs=pl.BlockSpec((tm, tn), lambda i, j: (i, j)),
        compiler_params=_params(("parallel", "arbitrary")),
        name="up_proj",
    )(h, w)


def _mm_ln_body(y_ref, w_ref, res_ref, g_ref, b_ref, of_ref, ob_ref, *, nk):
    kk = pl.program_id(1)

    @pl.when(kk == 0)
    def _():
        of_ref[...] = jnp.zeros_like(of_ref)

    of_ref[...] += _dot(y_ref[...], w_ref[...])

    slab = res_ref.shape[0]
    rs = pl.ds(pl.multiple_of(kk * slab, slab), slab)
    of_ref[rs, :] += ALPHA * res_ref[...]

    @pl.when(kk == nk - 1)
    def _():
        g = g_ref[...]
        b = b_ref[...]

        def rows(i, carry):
            r = pl.ds(pl.multiple_of(i * LN_ROWS, LN_ROWS), LN_ROWS)
            t = of_ref[r, :]
            mu = jnp.mean(t, axis=-1, keepdims=True)
            d = t - mu
            var = jnp.mean(d * d, axis=-1, keepdims=True)
            out = d * lax.rsqrt(var + EPS) * g + b
            of_ref[r, :] = out
            ob_ref[r, :] = out.astype(ob_ref.dtype)
            return carry

        lax.fori_loop(0, of_ref.shape[0] // LN_ROWS, rows, 0)


def _mm_ln(y, w, layer, res, g, b, *, tm=512, tk=1024, name):
    m, k = y.shape
    n = w.shape[2]
    tm = min(tm, m)
    nk = k // tk
    slab = tm // nk
    return pl.pallas_call(
        functools.partial(_mm_ln_body, nk=nk),
        out_shape=(jax.ShapeDtypeStruct((m, n), F32), jax.ShapeDtypeStruct((m, n), BF16)),
        grid=(m // tm, nk),
        in_specs=[pl.BlockSpec((tm, tk), lambda i, kk: (i, kk)),
                  pl.BlockSpec((None, tk, n), lambda i, kk: (layer, kk, 0)),
                  pl.BlockSpec((slab, n), lambda i, kk: (i * nk + kk, 0)),
                  pl.BlockSpec((1, n), lambda i, kk: (0, 0)),
                  pl.BlockSpec((1, n), lambda i, kk: (0, 0))],
        out_specs=(pl.BlockSpec((tm, n), lambda i, kk: (i, 0)),
                   pl.BlockSpec((tm, n), lambda i, kk: (i, 0))),
        compiler_params=_params(("parallel", "arbitrary")),
        name=name,
    )(y, w, res, g, b)


def _merge_body(ya_ref, yb_ref, yc_ref, wa_ref, wb_ref, wc_ref, ga_ref, gb_ref, gc_ref, o_ref):
    acc = ga_ref[...].astype(F32) * _dot(ya_ref[...], wa_ref[...])
    acc += gb_ref[...].astype(F32) * _dot(yb_ref[...], wb_ref[...])
    acc += gc_ref[...].astype(F32) * _dot(yc_ref[...], wc_ref[...])
    o_ref[...] = acc.astype(o_ref.dtype)


def _merge(ya, yb, yc, wa, wb, wc, layer, gates, *, tm=1024, tn=512):
    m = ya.shape[0]
    n = wa.shape[2]
    tm = min(tm, m)
    nb = n // tn

    def act(width):
        return pl.BlockSpec((tm, width), lambda i, j: (i, 0))

    def wt(width):
        return pl.BlockSpec((None, width, tn), lambda i, j: (layer, 0, j))

    def gate(branch):
        return pl.BlockSpec((tm, tn), lambda i, j: (i, branch * nb + j))

    return pl.pallas_call(
        _merge_body,
        out_shape=jax.ShapeDtypeStruct((m, n), BF16),
        grid=(m // tm, nb),
        in_specs=[act(ya.shape[1]), act(yb.shape[1]), act(yc.shape[1]),
                  wt(wa.shape[1]), wt(wb.shape[1]), wt(wc.shape[1]),
                  gate(0), gate(1), gate(2)],
        out_specs=pl.BlockSpec((tm, tn), lambda i, j: (i, j)),
        compiler_params=_params(("parallel", "arbitrary")),
        name="merge",
    )(ya, yb, yc, wa, wb, wc, gates, gates, gates)


def _attn_body(q_ref, k_ref, v_ref, lam_ref, g_ref, o_ref, qt_ref, vt_ref, m_ref, l_ref, acc_ref,
               *, tk, lam_init):
    seq = k_ref.shape[0]
    tq = q_ref.shape[0]
    nkv = seq // tk

    @pl.when(pl.program_id(2) == 0)
    def _():
        def transpose_values(i, carry):
            r = pl.ds(pl.multiple_of(i * tk, tk), tk)
            vt_ref[i] = v_ref[r, :].astype(F32).T.astype(BF16)
            return carry

        lax.fori_loop(0, nkv, transpose_values, 0)

    qt_ref[...] = q_ref[...].astype(F32).T.astype(BF16)
    m_ref[...] = jnp.full_like(m_ref, -jnp.inf)
    l_ref[...] = jnp.zeros_like(l_ref)
    acc_ref[...] = jnp.zeros_like(acc_ref)

    def step(i, carry):
        rows = pl.ds(pl.multiple_of(i * tk, tk), tk)
        vt = vt_ref[i]
        for c in range(2):
            kc = k_ref[rows, c * DA_DH:(c + 1) * DA_DH]
            for j in range(tq // ATT_STRIP):
                cols = slice(j * ATT_STRIP, (j + 1) * ATT_STRIP)
                st = _dot(kc, qt_ref[c * DA_DH:(c + 1) * DA_DH, cols])
                m_old = m_ref[c, :, cols]
                m_new = jnp.maximum(m_old, jnp.max(st, axis=0, keepdims=True))
                alpha = jnp.exp2(m_old - m_new)
                p = jnp.exp2(st - m_new)
                l_ref[c, :, cols] = alpha * l_ref[c, :, cols] + jnp.sum(p, axis=0, keepdims=True)
                acc_ref[c, :, cols] = alpha * acc_ref[c, :, cols] + _dot(vt, p.astype(BF16))
                m_ref[c, :, cols] = m_new
        return carry

    lax.fori_loop(0, nkv, step, 0, unroll=8)

    lp = lam_ref[...]
    lam = (jnp.exp(jnp.sum(lp[0:1] * lp[1:2], axis=-1, keepdims=True))
           - jnp.exp(jnp.sum(lp[2:3] * lp[3:4], axis=-1, keepdims=True)) + lam_init)
    o = acc_ref[0] / l_ref[0] - lam * (acc_ref[1] / l_ref[1])
    ms = jnp.mean(o * o, axis=0, keepdims=True)
    o = o * lax.rsqrt(ms + EPS) * g_ref[...] * (1.0 - lam_init)
    o_ref[...] = o.T.astype(o_ref.dtype)


def _attention(proj3, lam_params, norm_g_col, layer, *, tq=1024, tk=256):
    bsz, seq, _ = proj3.shape
    tq = min(tq, seq)
    tk = min(tk, seq)
    hw = 2 * DA_DH
    lam_init = 0.8 - 0.6 * math.exp(-0.3 * layer)
    return pl.pallas_call(
        functools.partial(_attn_body, tk=tk, lam_init=lam_init),
        out_shape=jax.ShapeDtypeStruct((bsz, seq, DA_V_WIDTH), BF16),
        grid=(bsz, DA_HEADS, seq // tq),
        in_specs=[pl.BlockSpec((None, tq, hw), lambda b, h, i: (b, i, COL_BQ // hw + h)),
                  pl.BlockSpec((None, seq, hw), lambda b, h, i: (b, 0, COL_BK // hw + h)),
                  pl.BlockSpec((None, seq, hw), lambda b, h, i: (b, 0, COL_BV // hw + h)),
                  pl.BlockSpec((4, DA_DH), lambda b, h, i: (0, 0)),
                  pl.BlockSpec((hw, 1), lambda b, h, i: (0, 0))],
        out_specs=pl.BlockSpec((None, tq, hw), lambda b, h, i: (b, i, h)),
        scratch_shapes=[pltpu.VMEM((hw, tq), BF16),
                        pltpu.VMEM((seq // tk, hw, tk), BF16),
                        pltpu.VMEM((2, 1, tq), F32),
                        pltpu.VMEM((2, 1, tq), F32),
                        pltpu.VMEM((2, hw, tq), F32)],
        compiler_params=_params(("parallel", "parallel", "arbitrary")),
        name="diff_attention",
    )(proj3, proj3, proj3, lam_params, norm_g_col)


def _log2(x):
    return jnp.log(x) * LOG2E


def _gla_gates(x, lb):
    xs = x * LOG2E
    e = jnp.exp2(-jnp.abs(xs))
    inv = 1.0 / (1.0 + e)
    sig_neg = jnp.where(xs >= 0.0, e, 1.0) * inv
    ls = jnp.minimum(xs, 0.0) - _log2(1.0 + e)
    if lb is None:
        return sig_neg, ls
    a = _log2(lb)
    b = _log2(1.0 - lb) + ls
    mx = jnp.maximum(a, b)
    log_f = mx + _log2(jnp.exp2(a - mx) + jnp.exp2(b - mx))
    return (1.0 - lb) * sig_neg, log_f


def _gla_chunk(q, x, v, state, lb, consts, reverse):
    tri, sel, blk_eq, row = consts
    cs, sub = GLA_CHUNK, GLA_SUB
    nsub = cs // sub
    k, g = _gla_gates(x, lb)
    b = jnp.dot(tri, g, precision=lax.Precision.HIGHEST, preferred_element_type=F32)
    edge = b[0:1] if reverse else b[cs - 1:cs]
    vb = v.astype(BF16)

    o = _dot_nt((q * jnp.exp2(b)).astype(BF16), state.astype(BF16))
    k_edge = (k * jnp.exp2(edge - b)).astype(BF16)
    new_state = jnp.exp2(edge) * state + _dot_tn(vb, k_edge)

    qs, ks = [], []
    for j in range(nsub):
        if (reverse and j == 0) or (not reverse and j == nsub - 1):
            continue
        lo, hi = j * sub, (j + 1) * sub
        anchor = b[lo:lo + 1] if reverse else b[hi - 1:hi]
        q_ok = (row < lo) if reverse else (row >= hi)
        k_ok = jnp.logical_and(row >= lo, row < hi)
        qs.append(jnp.where(q_ok, q * jnp.exp2(jnp.minimum(b - anchor, 0.0)), 0.0).astype(BF16))
        ks.append(jnp.where(k_ok, k * jnp.exp2(jnp.minimum(anchor - b, 0.0)), 0.0).astype(BF16))
    a_mat = _dot_nt(jnp.concatenate(qs, axis=1), jnp.concatenate(ks, axis=1))

    sub_row = lax.broadcasted_iota(jnp.int32, (sub, 1), 0)
    cols = []
    for s in range(sub):
        keep = (sub_row <= s) if reverse else (sub_row >= s)
        parts = []
        for j in range(nsub):
            lo = j * sub
            bb = b[lo:lo + sub]
            w = jnp.exp2(jnp.minimum(bb - b[lo + s:lo + s + 1], 0.0))
            parts.append(jnp.where(keep, q[lo:lo + sub] * k[lo + s:lo + s + 1] * w, 0.0).astype(BF16))
        cols.append(jnp.concatenate(parts, axis=0))
    diag = _dot(jnp.concatenate(cols, axis=1), sel)
    a_mat = a_mat + jnp.where(blk_eq, diag, 0.0)

    o = o + _dot(a_mat.astype(BF16), vb)
    return o, new_state


def _gla_body(q_ref, ff_ref, fb_ref, i_ref, g_ref, lbraw_ref, ng_ref, o_ref,
              of_ref, ob_ref, st_ref, *, layer):
    seq = q_ref.shape[0]
    cs, sub = GLA_CHUNK, GLA_SUB
    nc = seq // cs

    if layer == 0:
        lbs = (None, None)
    else:
        raw = lbraw_ref[...]
        e = jnp.exp(raw - jnp.max(raw, axis=0, keepdims=True))
        p = e / jnp.sum(e, axis=0, keepdims=True)
        lb2 = jnp.sum(p[1:layer + 1], axis=0)
        lbs = (lb2[0:1], lb2[1:2])

    r_i = lax.broadcasted_iota(jnp.int32, (cs, cs), 0)
    c_i = lax.broadcasted_iota(jnp.int32, (cs, cs), 1)
    tri_f = (c_i <= r_i).astype(F32)
    tri_b = (c_i >= r_i).astype(F32)
    blk_eq = (r_i // sub) == (c_i // sub)
    sr = lax.broadcasted_iota(jnp.int32, (sub * HG_DK, cs), 0)
    sc = lax.broadcasted_iota(jnp.int32, (sub * HG_DK, cs), 1)
    sel = ((sr // HG_DK) == (sc % sub)).astype(BF16)
    row = lax.broadcasted_iota(jnp.int32, (cs, 1), 0)

    st_ref[...] = jnp.zeros_like(st_ref)

    def step(i, carry):
        rf = pl.ds(pl.multiple_of(i * cs, cs), cs)
        rb = pl.ds(pl.multiple_of((nc - 1 - i) * cs, cs), cs)
        o_f, s_f = _gla_chunk(q_ref[rf, :].astype(F32), ff_ref[rf, :].astype(F32), i_ref[rf, :].astype(F32),
                              st_ref[0], lbs[0], (tri_f, sel, blk_eq, row), False)
        of_ref[rf, :] = o_f
        st_ref[0] = s_f
        o_b, s_b = _gla_chunk(q_ref[rb, :].astype(F32), fb_ref[rb, :].astype(F32), i_ref[rb, :].astype(F32),
                              st_ref[1], lbs[1], (tri_b, sel, blk_eq, row), True)
        ob_ref[rb, :] = o_b
        st_ref[1] = s_b
        return carry

    lax.fori_loop(0, nc, step, 0, unroll=GLA_UNROLL)

    tr = min(512, seq)
    ng = ng_ref[...]

    def fin(i, carry):
        r = pl.ds(pl.multiple_of(i * tr, tr), tr)
        o = of_ref[r, :] + ob_ref[r, :]
        ms = jnp.mean(o * o, axis=-1, keepdims=True)
        y = o * lax.rsqrt(ms + EPS) * ng
        gate = g_ref[r, :].astype(F32)
        o_ref[r, :] = (y * gate * _sigmoid(gate)).astype(o_ref.dtype)
        return carry

    lax.fori_loop(0, seq // tr, fin, 0)


def _hgrn2(proj3, lb_raw, norm_g, layer):
    bsz, seq, _ = proj3.shape

    def col(base):
        return pl.BlockSpec((None, seq, HG_DK), lambda b, h: (b, 0, base // HG_DK + h))

    return pl.pallas_call(
        functools.partial(_gla_body, layer=layer),
        out_shape=jax.ShapeDtypeStruct((bsz, seq, HG_WIDTH), BF16),
        grid=(bsz, HG_HEADS),
        in_specs=[col(COL_AQ), col(COL_AFF), col(COL_AFB), col(COL_AI), col(COL_AG),
                  pl.BlockSpec((DEPTH, 2, HG_DK), lambda b, h: (0, 0, h)),
                  pl.BlockSpec((1, HG_DK), lambda b, h: (0, h))],
        out_specs=pl.BlockSpec((None, seq, HG_DK), lambda b, h: (b, 0, h)),
        scratch_shapes=[pltpu.VMEM((seq, HG_DK), F32),
                        pltpu.VMEM((seq, HG_DK), F32),
                        pltpu.VMEM((2, HG_DK, HG_DK), F32)],
        compiler_params=_params(("parallel", "parallel")),
        name="hgrn2",
    )(proj3, proj3, proj3, proj3, proj3, lb_raw, norm_g)


def _gelu(x):
    return 0.5 * x * (1.0 + jnp.tanh(math.sqrt(2.0 / math.pi) * (x + 0.044715 * (x * x * x))))


def _sgu_body(u_ref, v_ref, ng_ref, nb_ref, ws_ref, bs_ref, o_ref):
    tm = u_ref.shape[0]
    v = _gelu(v_ref[...].astype(F32))
    mu = jnp.mean(v, axis=-1, keepdims=True)
    d = v - mu
    var = jnp.mean(d * d, axis=-1, keepdims=True)
    vn = (d * lax.rsqrt(var + EPS) * ng_ref[...] + nb_ref[...]).astype(BF16)
    bs = bs_ref[...]
    for c in range(tm // SG_CHUNK):
        r0 = c * SG_CHUNK
        for g in range(SG_GROUPS):
            c0 = g * SG_GDIM
            mixed = _dot(ws_ref[g], vn[r0:r0 + SG_CHUNK, c0:c0 + SG_GDIM]) + bs[:, g:g + 1]
            u = _gelu(u_ref[r0:r0 + SG_CHUNK, c0:c0 + SG_GDIM].astype(F32))
            o_ref[r0:r0 + SG_CHUNK, c0:c0 + SG_GDIM] = (u * mixed).astype(o_ref.dtype)


def _spatial_gating(proj, norm_g, norm_b, w_s, b_s_t, *, tm=256):
    m = proj.shape[0]
    return pl.pallas_call(
        _sgu_body,
        out_shape=jax.ShapeDtypeStruct((m, SG_WIDTH), BF16),
        grid=(m // tm,),
        in_specs=[pl.BlockSpec((tm, SG_WIDTH), lambda i: (i, COL_CU // SG_WIDTH)),
                  pl.BlockSpec((tm, SG_WIDTH), lambda i: (i, COL_CV // SG_WIDTH)),
                  pl.BlockSpec((1, SG_WIDTH), lambda i: (0, 0)),
                  pl.BlockSpec((1, SG_WIDTH), lambda i: (0, 0)),
                  pl.BlockSpec((SG_GROUPS, SG_CHUNK, SG_CHUNK), lambda i: (0, 0, 0)),
                  pl.BlockSpec((SG_CHUNK, SG_GROUPS), lambda i: (0, 0))],
        out_specs=pl.BlockSpec((tm, SG_WIDTH), lambda i: (i, 0)),
        compiler_params=_params(("parallel",)),
        name="spatial_gating",
    )(proj, proj, norm_g, norm_b, w_s, b_s_t)


def _rotary_tables(seq):
    half = ROT_DIM // 2
    pos = jnp.arange(seq, dtype=F32)
    freqs = ROPE_THETA ** (-jnp.arange(0, ROT_DIM, 2, dtype=F32) / ROT_DIM)
    ang = pos[:, None] * freqs[None, :]
    cos, sin = jnp.cos(ang), jnp.sin(ang)
    rest = LANES - ROT_DIM
    c = jnp.concatenate([cos, cos, jnp.ones((seq, rest), F32)], axis=1)
    s1 = jnp.concatenate([jnp.zeros((seq, half), F32), sin, jnp.zeros((seq, rest), F32)], axis=1)
    s2 = jnp.concatenate([-sin, jnp.zeros((seq, LANES - half), F32)], axis=1)
    return c, s1, s2


def kernel(x, w_in, hg_lb_raw, hg_norm_g, da_lambda, da_norm_g, sg_norm_g, sg_norm_b, sg_w_s, sg_b_s, w_branch_a, w_branch_b, w_branch_c, w_gate, b_gate, w_out, ln1_g, ln1_b, w_up, w_down, ln2_g, ln2_b):
    bsz, seq, d = x.shape
    m = bsz * seq
    rot_c, rot_s1, rot_s2 = _rotary_tables(seq)
    xf = x.reshape(m, d)
    xb = xf.astype(BF16)
    wa_b, wb_b, wc_b = (w.astype(BF16) for w in (w_branch_a, w_branch_b, w_branch_c))
    w_out_b = w_out.astype(BF16)
    w_down_b = w_down.astype(BF16)
    for layer in range(DEPTH):
        proj = _inproj(xb, w_in, layer, rot_c, rot_s1, rot_s2, seq)
        gates = _gate_proj(xb, w_gate, layer, b_gate[layer].reshape(1, -1))
        proj3 = proj.reshape(bsz, seq, D_IN)
        y_a = _hgrn2(proj3, hg_lb_raw, hg_norm_g[layer].reshape(1, -1), layer)
        y_b = _attention(proj3, da_lambda[layer], da_norm_g[layer].reshape(-1, 1), layer)
        y_c = _spatial_gating(proj, sg_norm_g[layer].reshape(1, -1), sg_norm_b[layer].reshape(1, -1),
                              sg_w_s[layer].astype(BF16), sg_b_s[layer].T)
        merged = _merge(y_a.reshape(m, HG_WIDTH), y_b.reshape(m, DA_V_WIDTH), y_c,
                        wa_b, wb_b, wc_b, layer, gates)
        xf, xb = _mm_ln(merged, w_out_b, layer, xf, ln1_g[layer].reshape(1, -1),
                        ln1_b[layer].reshape(1, -1), name="out_proj_ln")
        hid = _up_proj(xb, w_up, layer)
        xf, xb = _mm_ln(hid, w_down_b, layer, xf, ln2_g[layer].reshape(1, -1),
                        ln2_b[layer].reshape(1, -1), name="down_proj_ln")
    return xf.reshape(bsz, seq, d)
```

```python
import functools
import math

import jax
import jax.numpy as jnp
from jax import lax
from jax.experimental import pallas as pl
from jax.experimental.pallas import tpu as pltpu

D_MODEL = 4096
DEPTH = 2
HG_WIDTH = D_MODEL // 4
HG_DK = 128
HG_HEADS = HG_WIDTH // HG_DK
DA_DH = 128
DA_HEADS = D_MODEL // (4 * DA_DH)
DA_QK_WIDTH = DA_HEADS * 2 * DA_DH
DA_V_WIDTH = DA_HEADS * 2 * DA_DH
ROPE_THETA = 500000.0
ROT_DIM = DA_DH // 4
SG_CHUNK = 128
SG_WIDTH = D_MODEL // 4
SG_GDIM = 128
SG_GROUPS = SG_WIDTH // SG_GDIM
N_BRANCH = 3
D_FF = 4 * D_MODEL
ALPHA = (2.0 * DEPTH) ** 0.25
EPS = 1e-5
D_IN = 5 * HG_WIDTH + 2 * DA_QK_WIDTH + DA_V_WIDTH + 2 * SG_WIDTH

COL_AQ, COL_AFF, COL_AFB, COL_AI, COL_AG = (i * HG_WIDTH for i in range(5))
COL_BQ = 5 * HG_WIDTH
COL_BK = COL_BQ + DA_QK_WIDTH
COL_BV = COL_BK + DA_QK_WIDTH
COL_CU = COL_BV + DA_V_WIDTH
COL_CV = COL_CU + SG_WIDTH

LANES = 128
VMEM_LIMIT = 56 * 1024 * 1024
GLA_CHUNK = 64
GLA_SUB = 16
LN_ROWS = 64
ATT_STRIP = 256
GLA_UNROLL = 4
LOG2E = math.log2(math.e)

F32 = jnp.float32
BF16 = jnp.bfloat16


def _params(sem):
    return pltpu.CompilerParams(dimension_semantics=sem, vmem_limit_bytes=VMEM_LIMIT)


def _dot(a, b):
    return jnp.dot(a, b, preferred_element_type=F32)


def _dot_nt(a, b):
    return lax.dot_general(a, b, (((1,), (1,)), ((), ())), preferred_element_type=F32)


def _dot_tn(a, b):
    return lax.dot_general(a, b, (((0,), (0,)), ((), ())), preferred_element_type=F32)


def _sigmoid(x):
    return 1.0 / (1.0 + jnp.exp(-x))


def _inproj_body(x_ref, w_ref, c_ref, s1_ref, s2_ref, o_ref, *, tn):
    j = pl.program_id(1)
    acc = _dot(x_ref[...], w_ref[...].astype(BF16))
    first_rot = COL_BQ // tn
    first_k = COL_BK // tn
    end_rot = COL_BV // tn
    is_rot = jnp.logical_and(j >= first_rot, j < end_rot)

    @pl.when(is_rot)
    def _():
        scale = jnp.where(j < first_k, LOG2E * DA_DH ** -0.5, 1.0).astype(F32)
        c = c_ref[...]
        s1 = s1_ref[...]
        s2 = s2_ref[...]
        for g in range(tn // LANES):
            t = acc[:, g * LANES:(g + 1) * LANES]
            r = (t * c + pltpu.roll(t, ROT_DIM // 2, 1) * s1
                 + pltpu.roll(t, LANES - ROT_DIM // 2, 1) * s2)
            o_ref[:, g * LANES:(g + 1) * LANES] = (r * scale).astype(o_ref.dtype)

    @pl.when(jnp.logical_not(is_rot))
    def _():
        o_ref[...] = acc.astype(o_ref.dtype)


def _inproj(h, w, layer, rot_c, rot_s1, rot_s2, seq, *, tm=1024, tn=512):
    m, k = h.shape
    n = w.shape[2]
    tm = min(tm, seq)
    nseq = seq // tm
    tab = pl.BlockSpec((tm, LANES), lambda i, j: (i % nseq, 0))
    return pl.pallas_call(
        functools.partial(_inproj_body, tn=tn),
        out_shape=jax.ShapeDtypeStruct((m, n), BF16),
        grid=(m // tm, n // tn),
        in_specs=[pl.BlockSpec((tm, k), lambda i, j: (i, 0)),
                  pl.BlockSpec((None, k, tn), lambda i, j: (layer, 0, j)),
                  tab, tab, tab],
        out_specs=pl.BlockSpec((tm, tn), lambda i, j: (i, j)),
        compiler_params=_params(("parallel", "arbitrary")),
        name="inproj",
    )(h, w, rot_c, rot_s1, rot_s2)


def _gate_body(x_ref, w_ref, b_ref, o_ref):
    acc = _dot(x_ref[...], w_ref[...].astype(BF16)) + b_ref[...]
    o_ref[...] = _sigmoid(acc).astype(o_ref.dtype)


def _gate_proj(h, w, layer, b, *, tm=1024, tn=512):
    m, k = h.shape
    n = w.shape[2]
    tm = min(tm, m)
    return pl.pallas_call(
        _gate_body,
        out_shape=jax.ShapeDtypeStruct((m, n), BF16),
        grid=(m // tm, n // tn),
        in_specs=[pl.BlockSpec((tm, k), lambda i, j: (i, 0)),
                  pl.BlockSpec((None, k, tn), lambda i, j: (layer, 0, j)),
                  pl.BlockSpec((1, tn), lambda i, j: (0, j))],
        out_specs=pl.BlockSpec((tm, tn), lambda i, j: (i, j)),
        compiler_params=_params(("parallel", "arbitrary")),
        name="gate_proj",
    )(h, w, b)


def _up_body(x_ref, w_ref, o_ref):
    r = jnp.maximum(_dot(x_ref[...], w_ref[...].astype(BF16)), 0.0)
    o_ref[...] = (r * r).astype(o_ref.dtype)


def _up_proj(h, w, layer, *, tm=1024, tn=512):
    m, k = h.shape
    n = w.shape[2]
    tm = min(tm, m)
    return pl.pallas_call(
        _up_body,
        out_shape=jax.ShapeDtypeStruct((m, n), BF16),
        grid=(m // tm, n // tn),
        in_specs=[pl.BlockSpec((tm, k), lambda i, j: (i, 0)),
                  pl.BlockSpec((None, k, tn), lambda i, j: (layer, 0, j))],
        out_specs=pl.BlockSpec((tm, tn), lambda i, j: (i, j)),
        compiler_params=_params(("parallel", "arbitrary")),
        name="up_proj",
    )(h, w)


def _mm_ln_body(y_ref, w_ref, res_ref, g_ref, b_ref, of_ref, ob_ref, acc_a, acc_b, *, nk, nt):
    i = pl.program_id(0)
    kk = pl.program_id(1)
    slab = res_ref.shape[0]
    rows = min(slab, LN_ROWS)

    def normalise(prev_ref):
        g = g_ref[...]
        b = b_ref[...]
        for r0 in range(0, slab, rows):
            rs = pl.ds(pl.multiple_of(kk * slab + r0, rows), rows)
            t = prev_ref[rs, :] + ALPHA * res_ref[r0:r0 + rows, :]
            prev_ref[rs, :] = jnp.zeros((rows, prev_ref.shape[1]), F32)
            mu = jnp.mean(t, axis=-1, keepdims=True)
            d = t - mu
            var = jnp.mean(d * d, axis=-1, keepdims=True)
            out = d * lax.rsqrt(var + EPS) * g + b
            of_ref[r0:r0 + rows, :] = out
            ob_ref[r0:r0 + rows, :] = out.astype(ob_ref.dtype)

    @pl.when(jnp.logical_and(i == 0, kk == 0))
    def _():
        acc_a[...] = jnp.zeros_like(acc_a)
        acc_b[...] = jnp.zeros_like(acc_b)

    for parity, (cur_ref, prev_ref) in enumerate(((acc_a, acc_b), (acc_b, acc_a))):
        @pl.when(jnp.logical_and(i < nt, i % 2 == parity))
        def _(cur_ref=cur_ref, prev_ref=prev_ref):
            cur_ref[...] += _dot(y_ref[...], w_ref[...])
            normalise(prev_ref)

    @pl.when(i == nt)
    def _():
        normalise(acc_a if (nt - 1) % 2 == 0 else acc_b)


def _mm_ln(y, w, layer, res, g, b, *, tm=512, tk=1024, name):
    m, k = y.shape
    n = w.shape[2]
    tm = min(tm, m)
    nk = k // tk
    nt = m // tm
    slab = tm // nk

    def lagged(i, kk):
        return (jnp.where(i == 0, 0, (i - 1) * nk + kk), 0)

    def last_k(i, kk):
        return jnp.where(i < nt, kk, nk - 1)

    return pl.pallas_call(
        functools.partial(_mm_ln_body, nk=nk, nt=nt),
        out_shape=(jax.ShapeDtypeStruct((m, n), F32), jax.ShapeDtypeStruct((m, n), BF16)),
        grid=(nt + 1, nk),
        in_specs=[pl.BlockSpec((tm, tk), lambda i, kk: (jnp.minimum(i, nt - 1), last_k(i, kk))),
                  pl.BlockSpec((None, tk, n), lambda i, kk: (layer, last_k(i, kk), 0)),
                  pl.BlockSpec((slab, n), lagged),
                  pl.BlockSpec((1, n), lambda i, kk: (0, 0)),
                  pl.BlockSpec((1, n), lambda i, kk: (0, 0))],
        out_specs=(pl.BlockSpec((slab, n), lagged),
                   pl.BlockSpec((slab, n), lagged)),
        scratch_shapes=[pltpu.VMEM((tm, n), F32), pltpu.VMEM((tm, n), F32)],
        compiler_params=_params(("arbitrary", "arbitrary")),
        name=name,
    )(y, w, res, g, b)


def _merge_body(ya_ref, yb_ref, yc_ref, wa_ref, wb_ref, wc_ref, ga_ref, gb_ref, gc_ref, o_ref):
    acc = ga_ref[...].astype(F32) * _dot(ya_ref[...], wa_ref[...])
    acc += gb_ref[...].astype(F32) * _dot(yb_ref[...], wb_ref[...])
    acc += gc_ref[...].astype(F32) * _dot(yc_ref[...], wc_ref[...])
    o_ref[...] = acc.astype(o_ref.dtype)


def _merge(ya, yb, yc, wa, wb, wc, layer, gates, *, tm=1024, tn=512):
    m = ya.shape[0]
    n = wa.shape[2]
    tm = min(tm, m)
    nb = n // tn

    def act(width):
        return pl.BlockSpec((tm, width), lambda i, j: (i, 0))

    def wt(width):
        return pl.BlockSpec((None, width, tn), lambda i, j: (layer, 0, j))

    def gate(branch):
        return pl.BlockSpec((tm, tn), lambda i, j: (i, branch * nb + j))

    return pl.pallas_call(
        _merge_body,
        out_shape=jax.ShapeDtypeStruct((m, n), BF16),
        grid=(m // tm, nb),
        in_specs=[act(ya.shape[1]), act(yb.shape[1]), act(yc.shape[1]),
                  wt(wa.shape[1]), wt(wb.shape[1]), wt(wc.shape[1]),
                  gate(0), gate(1), gate(2)],
        out_specs=pl.BlockSpec((tm, tn), lambda i, j: (i, j)),
        compiler_params=_params(("parallel", "arbitrary")),
        name="merge",
    )(ya, yb, yc, wa, wb, wc, gates, gates, gates)


def _attn_body(q_ref, k_ref, v_ref, lam_ref, g_ref, o_ref, qt_ref, vt_ref, m_ref, l_ref, acc_ref,
               *, tk, lam_init):
    seq = k_ref.shape[0]
    tq = q_ref.shape[0]
    nkv = seq // tk

    @pl.when(pl.program_id(2) == 0)
    def _():
        def transpose_values(i, carry):
            r = pl.ds(pl.multiple_of(i * tk, tk), tk)
            vt_ref[i] = v_ref[r, :].astype(F32).T.astype(BF16)
            return carry

        lax.fori_loop(0, nkv, transpose_values, 0)

    qt_ref[...] = q_ref[...].astype(F32).T.astype(BF16)
    m_ref[...] = jnp.full_like(m_ref, -jnp.inf)
    l_ref[...] = jnp.zeros_like(l_ref)
    acc_ref[...] = jnp.zeros_like(acc_ref)

    def step(i, carry):
        rows = pl.ds(pl.multiple_of(i * tk, tk), tk)
        vt = vt_ref[i]
        for c in range(2):
            kc = k_ref[rows, c * DA_DH:(c + 1) * DA_DH]
            for j in range(tq // ATT_STRIP):
                cols = slice(j * ATT_STRIP, (j + 1) * ATT_STRIP)
                st = _dot(kc, qt_ref[c * DA_DH:(c + 1) * DA_DH, cols])
                m_old = m_ref[c, :, cols]
                m_new = jnp.maximum(m_old, jnp.max(st, axis=0, keepdims=True))
                alpha = jnp.exp2(m_old - m_new)
                p = jnp.exp2(st - m_new)
                l_ref[c, :, cols] = alpha * l_ref[c, :, cols] + jnp.sum(p, axis=0, keepdims=True)
                acc_ref[c, :, cols] = alpha * acc_ref[c, :, cols] + _dot(vt, p.astype(BF16))
                m_ref[c, :, cols] = m_new
        return carry

    lax.fori_loop(0, nkv, step, 0, unroll=16)

    lp = lam_ref[...]
    lam = (jnp.exp(jnp.sum(lp[0:1] * lp[1:2], axis=-1, keepdims=True))
           - jnp.exp(jnp.sum(lp[2:3] * lp[3:4], axis=-1, keepdims=True)) + lam_init)
    o = acc_ref[0] / l_ref[0] - lam * (acc_ref[1] / l_ref[1])
    ms = jnp.mean(o * o, axis=0, keepdims=True)
    o = o * lax.rsqrt(ms + EPS) * g_ref[...] * (1.0 - lam_init)
    o_ref[...] = o.T.astype(o_ref.dtype)


def _attention(proj3, lam_params, norm_g_col, layer, *, tq=1024, tk=256):
    bsz, seq, _ = proj3.shape
    tq = min(tq, seq)
    tk = min(tk, seq)
    hw = 2 * DA_DH
    lam_init = 0.8 - 0.6 * math.exp(-0.3 * layer)
    return pl.pallas_call(
        functools.partial(_attn_body, tk=tk, lam_init=lam_init),
        out_shape=jax.ShapeDtypeStruct((bsz, seq, DA_V_WIDTH), BF16),
        grid=(bsz, DA_HEADS, seq // tq),
        in_specs=[pl.BlockSpec((None, tq, hw), lambda b, h, i: (b, i, COL_BQ // hw + h)),
                  pl.BlockSpec((None, seq, hw), lambda b, h, i: (b, 0, COL_BK // hw + h)),
                  pl.BlockSpec((None, seq, hw), lambda b, h, i: (b, 0, COL_BV // hw + h)),
                  pl.BlockSpec((4, DA_DH), lambda b, h, i: (0, 0)),
                  pl.BlockSpec((hw, 1), lambda b, h, i: (0, 0))],
        out_specs=pl.BlockSpec((None, tq, hw), lambda b, h, i: (b, i, h)),
        scratch_shapes=[pltpu.VMEM((hw, tq), BF16),
                        pltpu.VMEM((seq // tk, hw, tk), BF16),
                        pltpu.VMEM((2, 1, tq), F32),
                        pltpu.VMEM((2, 1, tq), F32),
                        pltpu.VMEM((2, hw, tq), F32)],
        compiler_params=_params(("parallel", "parallel", "arbitrary")),
        name="diff_attention",
    )(proj3, proj3, proj3, lam_params, norm_g_col)


def _log2(x):
    return jnp.log(x) * LOG2E


def _gla_gates(x, lb):
    xs = x * LOG2E
    e = jnp.exp2(-jnp.abs(xs))
    inv = 1.0 / (1.0 + e)
    sig_neg = jnp.where(xs >= 0.0, e, 1.0) * inv
    ls = jnp.minimum(xs, 0.0) - _log2(1.0 + e)
    if lb is None:
        return sig_neg, ls
    a = _log2(lb)
    b = _log2(1.0 - lb) + ls
    mx = jnp.maximum(a, b)
    log_f = mx + _log2(jnp.exp2(a - mx) + jnp.exp2(b - mx))
    return (1.0 - lb) * sig_neg, log_f


def _gla_chunk(q, x, v, state, lb, consts, reverse):
    tri, sel, blk_eq, row = consts
    cs, sub = GLA_CHUNK, GLA_SUB
    nsub = cs // sub
    k, g = _gla_gates(x, lb)
    b = jnp.dot(tri, g, precision=lax.Precision.HIGHEST, preferred_element_type=F32)
    edge = b[0:1] if reverse else b[cs - 1:cs]
    vb = v.astype(BF16)

    o = _dot_nt((q * jnp.exp2(b)).astype(BF16), state.astype(BF16))
    k_edge = (k * jnp.exp2(edge - b)).astype(BF16)
    new_state = jnp.exp2(edge) * state + _dot_tn(vb, k_edge)

    qs, ks = [], []
    for j in range(nsub):
        if (reverse and j == 0) or (not reverse and j == nsub - 1):
            continue
        lo, hi = j * sub, (j + 1) * sub
        anchor = b[lo:lo + 1] if reverse else b[hi - 1:hi]
        q_ok = (row < lo) if reverse else (row >= hi)
        k_ok = jnp.logical_and(row >= lo, row < hi)
        qs.append(jnp.where(q_ok, q * jnp.exp2(jnp.minimum(b - anchor, 0.0)), 0.0).astype(BF16))
        ks.append(jnp.where(k_ok, k * jnp.exp2(jnp.minimum(anchor - b, 0.0)), 0.0).astype(BF16))
    a_mat = _dot_nt(jnp.concatenate(qs, axis=1), jnp.concatenate(ks, axis=1))

    sub_row = lax.broadcasted_iota(jnp.int32, (sub, 1), 0)
    cols = []
    for s in range(sub):
        keep = (sub_row <= s) if reverse else (sub_row >= s)
        parts = []
        for j in range(nsub):
            lo = j * sub
            bb = b[lo:lo + sub]
            w = jnp.exp2(jnp.minimum(bb - b[lo + s:lo + s + 1], 0.0))
            parts.append(jnp.where(keep, q[lo:lo + sub] * k[lo + s:lo + s + 1] * w, 0.0).astype(BF16))
        cols.append(jnp.concatenate(parts, axis=0))
    diag = _dot(jnp.concatenate(cols, axis=1), sel)
    a_mat = a_mat + jnp.where(blk_eq, diag, 0.0)

    o = o + _dot(a_mat.astype(BF16), vb)
    return o, new_state


def _gla_body(q_ref, ff_ref, fb_ref, i_ref, g_ref, lbraw_ref, ng_ref, o_ref,
              of_ref, ob_ref, st_ref, *, layer):
    seq = q_ref.shape[0]
    cs, sub = GLA_CHUNK, GLA_SUB
    nc = seq // cs

    if layer == 0:
        lbs = (None, None)
    else:
        raw = lbraw_ref[...]
        e = jnp.exp(raw - jnp.max(raw, axis=0, keepdims=True))
        p = e / jnp.sum(e, axis=0, keepdims=True)
        lb2 = jnp.sum(p[1:layer + 1], axis=0)
        lbs = (lb2[0:1], lb2[1:2])

    r_i = lax.broadcasted_iota(jnp.int32, (cs, cs), 0)
    c_i = lax.broadcasted_iota(jnp.int32, (cs, cs), 1)
    tri_f = (c_i <= r_i).astype(F32)
    tri_b = (c_i >= r_i).astype(F32)
    blk_eq = (r_i // sub) == (c_i // sub)
    sr = lax.broadcasted_iota(jnp.int32, (sub * HG_DK, cs), 0)
    sc = lax.broadcasted_iota(jnp.int32, (sub * HG_DK, cs), 1)
    sel = ((sr // HG_DK) == (sc % sub)).astype(BF16)
    row = lax.broadcasted_iota(jnp.int32, (cs, 1), 0)

    st_ref[...] = jnp.zeros_like(st_ref)

    def step(i, carry):
        rf = pl.ds(pl.multiple_of(i * cs, cs), cs)
        rb = pl.ds(pl.multiple_of((nc - 1 - i) * cs, cs), cs)
        o_f, s_f = _gla_chunk(q_ref[rf, :].astype(F32), ff_ref[rf, :].astype(F32), i_ref[rf, :].astype(F32),
                              st_ref[0], lbs[0], (tri_f, sel, blk_eq, row), False)
        of_ref[rf, :] = o_f
        st_ref[0] = s_f
        o_b, s_b = _gla_chunk(q_ref[rb, :].astype(F32), fb_ref[rb, :].astype(F32), i_ref[rb, :].astype(F32),
                              st_ref[1], lbs[1], (tri_b, sel, blk_eq, row), True)
        ob_ref[rb, :] = o_b
        st_ref[1] = s_b
        return carry

    lax.fori_loop(0, nc, step, 0, unroll=GLA_UNROLL)

    tr = min(512, seq)
    ng = ng_ref[...]

    def fin(i, carry):
        r = pl.ds(pl.multiple_of(i * tr, tr), tr)
        o = of_ref[r, :] + ob_ref[r, :]
        ms = jnp.mean(o * o, axis=-1, keepdims=True)
        y = o * lax.rsqrt(ms + EPS) * ng
        gate = g_ref[r, :].astype(F32)
        o_ref[r, :] = (y * gate * _sigmoid(gate)).astype(o_ref.dtype)
        return carry

    lax.fori_loop(0, seq // tr, fin, 0)


def _hgrn2(proj3, lb_raw, norm_g, layer):
    bsz, seq, _ = proj3.shape

    def col(base):
        return pl.BlockSpec((None, seq, HG_DK), lambda b, h: (b, 0, base // HG_DK + h))

    return pl.pallas_call(
        functools.partial(_gla_body, layer=layer),
        out_shape=jax.ShapeDtypeStruct((bsz, seq, HG_WIDTH), BF16),
        grid=(bsz, HG_HEADS),
        in_specs=[col(COL_AQ), col(COL_AFF), col(COL_AFB), col(COL_AI), col(COL_AG),
                  pl.BlockSpec((DEPTH, 2, HG_DK), lambda b, h: (0, 0, h)),
                  pl.BlockSpec((1, HG_DK), lambda b, h: (0, h))],
        out_specs=pl.BlockSpec((None, seq, HG_DK), lambda b, h: (b, 0, h)),
        scratch_shapes=[pltpu.VMEM((seq, HG_DK), F32),
                        pltpu.VMEM((seq, HG_DK), F32),
                        pltpu.VMEM((2, HG_DK, HG_DK), F32)],
        compiler_params=_params(("parallel", "parallel")),
        name="hgrn2",
    )(proj3, proj3, proj3, proj3, proj3, lb_raw, norm_g)


def _gelu(x):
    return 0.5 * x * (1.0 + jnp.tanh(math.sqrt(2.0 / math.pi) * (x + 0.044715 * (x * x * x))))


def _sgu_body(u_ref, v_ref, ng_ref, nb_ref, ws_ref, bs_ref, o_ref):
    tm = u_ref.shape[0]
    v = _gelu(v_ref[...].astype(F32))
    mu = jnp.mean(v, axis=-1, keepdims=True)
    d = v - mu
    var = jnp.mean(d * d, axis=-1, keepdims=True)
    vn = (d * lax.rsqrt(var + EPS) * ng_ref[...] + nb_ref[...]).astype(BF16)
    bs = bs_ref[...]
    for c in range(tm // SG_CHUNK):
        r0 = c * SG_CHUNK
        for g in range(SG_GROUPS):
            c0 = g * SG_GDIM
            mixed = _dot(ws_ref[g], vn[r0:r0 + SG_CHUNK, c0:c0 + SG_GDIM]) + bs[:, g:g + 1]
            u = _gelu(u_ref[r0:r0 + SG_CHUNK, c0:c0 + SG_GDIM].astype(F32))
            o_ref[r0:r0 + SG_CHUNK, c0:c0 + SG_GDIM] = (u * mixed).astype(o_ref.dtype)


def _spatial_gating(proj, norm_g, norm_b, w_s, b_s_t, *, tm=256):
    m = proj.shape[0]
    return pl.pallas_call(
        _sgu_body,
        out_shape=jax.ShapeDtypeStruct((m, SG_WIDTH), BF16),
        grid=(m // tm,),
        in_specs=[pl.BlockSpec((tm, SG_WIDTH), lambda i: (i, COL_CU // SG_WIDTH)),
                  pl.BlockSpec((tm, SG_WIDTH), lambda i: (i, COL_CV // SG_WIDTH)),
                  pl.BlockSpec((1, SG_WIDTH), lambda i: (0, 0)),
                  pl.BlockSpec((1, SG_WIDTH), lambda i: (0, 0)),
                  pl.BlockSpec((SG_GROUPS, SG_CHUNK, SG_CHUNK), lambda i: (0, 0, 0)),
                  pl.BlockSpec((SG_CHUNK, SG_GROUPS), lambda i: (0, 0))],
        out_specs=pl.BlockSpec((tm, SG_WIDTH), lambda i: (i, 0)),
        compiler_params=_params(("parallel",)),
        name="spatial_gating",
    )(proj, proj, norm_g, norm_b, w_s, b_s_t)


def _rotary_tables(seq):
    half = ROT_DIM // 2
    pos = jnp.arange(seq, dtype=F32)
    freqs = ROPE_THETA ** (-jnp.arange(0, ROT_DIM, 2, dtype=F32) / ROT_DIM)
    ang = pos[:, None] * freqs[None, :]
    cos, sin = jnp.cos(ang), jnp.sin(ang)
    rest = LANES - ROT_DIM
    c = jnp.concatenate([cos, cos, jnp.ones((seq, rest), F32)], axis=1)
    s1 = jnp.concatenate([jnp.zeros((seq, half), F32), sin, jnp.zeros((seq, rest), F32)], axis=1)
    s2 = jnp.concatenate([-sin, jnp.zeros((seq, LANES - half), F32)], axis=1)
    return c, s1, s2


def kernel(x, w_in, hg_lb_raw, hg_norm_g, da_lambda, da_norm_g, sg_norm_g, sg_norm_b, sg_w_s, sg_b_s, w_branch_a, w_branch_b, w_branch_c, w_gate, b_gate, w_out, ln1_g, ln1_b, w_up, w_down, ln2_g, ln2_b):
    bsz, seq, d = x.shape
    m = bsz * seq
    rot_c, rot_s1, rot_s2 = _rotary_tables(seq)
    xf = x.reshape(m, d)
    xb = xf.astype(BF16)
    wa_b, wb_b, wc_b = (w.astype(BF16) for w in (w_branch_a, w_branch_b, w_branch_c))
    w_out_b = w_out.astype(BF16)
    w_down_b = w_down.astype(BF16)
    for layer in range(DEPTH):
        proj = _inproj(xb, w_in, layer, rot_c, rot_s1, rot_s2, seq)
        gates = _gate_proj(xb, w_gate, layer, b_gate[layer].reshape(1, -1))
        proj3 = proj.reshape(bsz, seq, D_IN)
        y_a = _hgrn2(proj3, hg_lb_raw, hg_norm_g[layer].reshape(1, -1), layer)
        y_b = _attention(proj3, da_lambda[layer], da_norm_g[layer].reshape(-1, 1), layer)
        y_c = _spatial_gating(proj, sg_norm_g[layer].reshape(1, -1), sg_norm_b[layer].reshape(1, -1),
                              sg_w_s[layer].astype(BF16), sg_b_s[layer].T)
        merged = _merge(y_a.reshape(m, HG_WIDTH), y_b.reshape(m, DA_V_WIDTH), y_c,
                        wa_b, wb_b, wc_b, layer, gates)
        xf, xb = _mm_ln(merged, w_out_b, layer, xf, ln1_g[layer].reshape(1, -1),
                        ln1_b[layer].reshape(1, -1), name="out_proj_ln")
        hid = _up_proj(xb, w_up, layer)
        xf, xb = _mm_ln(hid, w_down_b, layer, xf, ln2_g[layer].reshape(1, -1),
                        ln2_b[layer].reshape(1, -1), name="down_proj_ln")
    return xf.reshape(bsz, seq, d)
```

```python
import functools
import math

import jax
import jax.numpy as jnp
from jax import lax
from jax.experimental import pallas as pl
from jax.experimental.pallas import tpu as pltpu

D_MODEL = 4096
DEPTH = 2
HG_WIDTH = D_MODEL // 4
HG_DK = 128
HG_HEADS = HG_WIDTH // HG_DK
DA_DH = 128
DA_HEADS = D_MODEL // (4 * DA_DH)
DA_QK_WIDTH = DA_HEADS * 2 * DA_DH
DA_V_WIDTH = DA_HEADS * 2 * DA_DH
ROPE_THETA = 500000.0
ROT_DIM = DA_DH // 4
SG_CHUNK = 128
SG_WIDTH = D_MODEL // 4
SG_GDIM = 128
SG_GROUPS = SG_WIDTH // SG_GDIM
N_BRANCH = 3
D_FF = 4 * D_MODEL
ALPHA = (2.0 * DEPTH) ** 0.25
EPS = 1e-5
D_IN = 5 * HG_WIDTH + 2 * DA_QK_WIDTH + DA_V_WIDTH + 2 * SG_WIDTH

COL_AQ, COL_AFF, COL_AFB, COL_AI, COL_AG = (i * HG_WIDTH for i in range(5))
COL_BQ = 5 * HG_WIDTH
COL_BK = COL_BQ + DA_QK_WIDTH
COL_BV = COL_BK + DA_QK_WIDTH
COL_CU = COL_BV + DA_V_WIDTH
COL_CV = COL_CU + SG_WIDTH

QK_Q, QK_K = 0, DA_QK_WIDTH
REST_WIDTH = D_IN - 2 * DA_QK_WIDTH
REST_BV = COL_BV - 2 * DA_QK_WIDTH
REST_CU = COL_CU - 2 * DA_QK_WIDTH
REST_CV = COL_CV - 2 * DA_QK_WIDTH

LANES = 128
VMEM_LIMIT = 56 * 1024 * 1024
GLA_CHUNK = 64
GLA_SUB = 16
LN_ROWS = 64
ATT_STRIP = 256
GLA_UNROLL = 8
LOG2E = math.log2(math.e)

F32 = jnp.float32
BF16 = jnp.bfloat16


def _params(sem):
    return pltpu.CompilerParams(dimension_semantics=sem, vmem_limit_bytes=VMEM_LIMIT)


def _dot(a, b):
    return jnp.dot(a, b, preferred_element_type=F32)


def _dot_nt(a, b):
    return lax.dot_general(a, b, (((1,), (1,)), ((), ())), preferred_element_type=F32)


def _dot_tn(a, b):
    return lax.dot_general(a, b, (((0,), (0,)), ((), ())), preferred_element_type=F32)


def _sigmoid(x):
    return 0.5 * jnp.tanh(0.5 * x) + 0.5


def _proj_qk_body(x_ref, w_ref, c_ref, s1_ref, s2_ref, o_ref):
    acc = _dot(x_ref[...], w_ref[...].astype(BF16))
    c = c_ref[...]
    s1 = s1_ref[...]
    s2 = s2_ref[...]
    for g in range(o_ref.shape[1] // LANES):
        t = acc[:, g * LANES:(g + 1) * LANES]
        r = (t * c + pltpu.roll(t, ROT_DIM // 2, 1) * s1
             + pltpu.roll(t, LANES - ROT_DIM // 2, 1) * s2)
        o_ref[:, g * LANES:(g + 1) * LANES] = r.astype(o_ref.dtype)


def _proj_qk(h, w, layer, rot_c, rot_s1, rot_s2, seq, *, tm=1024, tn=512):
    m, k = h.shape
    tm = min(tm, seq)
    nseq = seq // tm
    nq = DA_QK_WIDTH // tn
    first = COL_BQ // tn
    tab = pl.BlockSpec((None, tm, LANES), lambda i, j: (j // nq, i % nseq, 0))
    return pl.pallas_call(
        _proj_qk_body,
        out_shape=jax.ShapeDtypeStruct((m, 2 * DA_QK_WIDTH), BF16),
        grid=(m // tm, 2 * nq),
        in_specs=[pl.BlockSpec((tm, k), lambda i, j: (i, 0)),
                  pl.BlockSpec((None, k, tn), lambda i, j: (layer, 0, first + j)),
                  tab, tab, tab],
        out_specs=pl.BlockSpec((tm, tn), lambda i, j: (i, j)),
        compiler_params=_params(("parallel", "arbitrary")),
        name="proj_qk",
    )(h, w, rot_c, rot_s1, rot_s2)


def _proj_rest_body(x_ref, w_ref, o_ref):
    o_ref[...] = _dot(x_ref[...], w_ref[...].astype(BF16)).astype(o_ref.dtype)


def _proj_rest(h, w, layer, *, tm=1024, tn=512):
    m, k = h.shape
    tm = min(tm, m)
    first_skip = COL_BQ // tn
    skipped = 2 * DA_QK_WIDTH // tn
    return pl.pallas_call(
        _proj_rest_body,
        out_shape=jax.ShapeDtypeStruct((m, REST_WIDTH), BF16),
        grid=(m // tm, REST_WIDTH // tn),
        in_specs=[pl.BlockSpec((tm, k), lambda i, j: (i, 0)),
                  pl.BlockSpec((None, k, tn),
                               lambda i, j: (layer, 0, j + jnp.where(j >= first_skip, skipped, 0)))],
        out_specs=pl.BlockSpec((tm, tn), lambda i, j: (i, j)),
        compiler_params=_params(("parallel", "arbitrary")),
        name="proj_rest",
    )(h, w)


def _gate_body(x_ref, w_ref, b_ref, o_ref):
    acc = _dot(x_ref[...], w_ref[...].astype(BF16)) + b_ref[...]
    o_ref[...] = _sigmoid(acc).astype(o_ref.dtype)


def _gate_proj(h, w, layer, b, *, tm=1024, tn=512):
    m, k = h.shape
    n = w.shape[2]
    tm = min(tm, m)
    return pl.pallas_call(
        _gate_body,
        out_shape=jax.ShapeDtypeStruct((m, n), BF16),
        grid=(m // tm, n // tn),
        in_specs=[pl.BlockSpec((tm, k), lambda i, j: (i, 0)),
                  pl.BlockSpec((None, k, tn), lambda i, j: (layer, 0, j)),
                  pl.BlockSpec((1, tn), lambda i, j: (0, j))],
        out_specs=pl.BlockSpec((tm, tn), lambda i, j: (i, j)),
        compiler_params=_params(("parallel", "arbitrary")),
        name="gate_proj",
    )(h, w, b)


def _up_body(x_ref, w_ref, o_ref):
    r = jnp.maximum(_dot(x_ref[...], w_ref[...].astype(BF16)), 0.0)
    o_ref[...] = (r * r).astype(o_ref.dtype)


def _up_proj(h, w, layer, *, tm=1024, tn=512):
    m, k = h.shape
    n = w.shape[2]
    tm = min(tm, m)
    return pl.pallas_call(
        _up_body,
        out_shape=jax.ShapeDtypeStruct((m, n), BF16),
        grid=(m // tm, n // tn),
        in_specs=[pl.BlockSpec((tm, k), lambda i, j: (i, 0)),
                  pl.BlockSpec((None, k, tn), lambda i, j: (layer, 0, j))],
        out_specs=pl.BlockSpec((tm, tn), lambda i, j: (i, j)),
        compiler_params=_params(("parallel", "arbitrary")),
        name="up_proj",
    )(h, w)


def _mm_ln_body(y_ref, w_ref, res_ref, g_ref, b_ref, of_ref, ob_ref, acc_a, acc_b, *, nk, nt):
    i = pl.program_id(0)
    kk = pl.program_id(1)
    slab = res_ref.shape[0]
    rows = min(slab, LN_ROWS)

    def normalise(prev_ref):
        g = g_ref[...]
        b = b_ref[...]
        for r0 in range(0, slab, rows):
            rs = pl.ds(pl.multiple_of(kk * slab + r0, rows), rows)
            t = prev_ref[rs, :] + ALPHA * res_ref[r0:r0 + rows, :]
            prev_ref[rs, :] = jnp.zeros((rows, prev_ref.shape[1]), F32)
            mu = jnp.mean(t, axis=-1, keepdims=True)
            d = t - mu
            var = jnp.mean(d * d, axis=-1, keepdims=True)
            out = d * lax.rsqrt(var + EPS) * g + b
            of_ref[r0:r0 + rows, :] = out
            ob_ref[r0:r0 + rows, :] = out.astype(ob_ref.dtype)

    @pl.when(jnp.logical_and(i == 0, kk == 0))
    def _():
        acc_a[...] = jnp.zeros_like(acc_a)
        acc_b[...] = jnp.zeros_like(acc_b)

    for parity, (cur_ref, prev_ref) in enumerate(((acc_a, acc_b), (acc_b, acc_a))):
        @pl.when(jnp.logical_and(i < nt, i % 2 == parity))
        def _(cur_ref=cur_ref, prev_ref=prev_ref):
            cur_ref[...] += _dot(y_ref[...], w_ref[...])
            normalise(prev_ref)

    @pl.when(i == nt)
    def _():
        normalise(acc_a if (nt - 1) % 2 == 0 else acc_b)


def _mm_ln(y, w, layer, res, g, b, *, tm=512, tk=1024, name):
    m, k = y.shape
    n = w.shape[2]
    tm = min(tm, m)
    nk = k // tk
    nt = m // tm
    slab = tm // nk

    def lagged(i, kk):
        return (jnp.where(i == 0, 0, (i - 1) * nk + kk), 0)

    def last_k(i, kk):
        return jnp.where(i < nt, kk, nk - 1)

    return pl.pallas_call(
        functools.partial(_mm_ln_body, nk=nk, nt=nt),
        out_shape=(jax.ShapeDtypeStruct((m, n), F32), jax.ShapeDtypeStruct((m, n), BF16)),
        grid=(nt + 1, nk),
        in_specs=[pl.BlockSpec((tm, tk), lambda i, kk: (jnp.minimum(i, nt - 1), last_k(i, kk))),
                  pl.BlockSpec((None, tk, n), lambda i, kk: (layer, last_k(i, kk), 0)),
                  pl.BlockSpec((slab, n), lagged),
                  pl.BlockSpec((1, n), lambda i, kk: (0, 0)),
                  pl.BlockSpec((1, n), lambda i, kk: (0, 0))],
        out_specs=(pl.BlockSpec((slab, n), lagged),
                   pl.BlockSpec((slab, n), lagged)),
        scratch_shapes=[pltpu.VMEM((tm, n), F32), pltpu.VMEM((tm, n), F32)],
        compiler_params=_params(("arbitrary", "arbitrary")),
        name=name,
    )(y, w, res, g, b)


def _merge_body(ya_ref, yb_ref, yc_ref, wa_ref, wb_ref, wc_ref, ga_ref, gb_ref, gc_ref, o_ref):
    acc = ga_ref[...].astype(F32) * _dot(ya_ref[...], wa_ref[...])
    acc += gb_ref[...].astype(F32) * _dot(yb_ref[...], wb_ref[...])
    acc += gc_ref[...].astype(F32) * _dot(yc_ref[...], wc_ref[...])
    o_ref[...] = acc.astype(o_ref.dtype)


def _merge(ya, yb, yc, wa, wb, wc, layer, gates, *, tm=1024, tn=512):
    m = ya.shape[0]
    n = wa.shape[2]
    tm = min(tm, m)
    nb = n // tn

    def act(width):
        return pl.BlockSpec((tm, width), lambda i, j: (i, 0))

    def wt(width):
        return pl.BlockSpec((None, width, tn), lambda i, j: (layer, 0, j))

    def gate(branch):
        return pl.BlockSpec((tm, tn), lambda i, j: (i, branch * nb + j))

    return pl.pallas_call(
        _merge_body,
        out_shape=jax.ShapeDtypeStruct((m, n), BF16),
        grid=(m // tm, nb),
        in_specs=[act(ya.shape[1]), act(yb.shape[1]), act(yc.shape[1]),
                  wt(wa.shape[1]), wt(wb.shape[1]), wt(wc.shape[1]),
                  gate(0), gate(1), gate(2)],
        out_specs=pl.BlockSpec((tm, tn), lambda i, j: (i, j)),
        compiler_params=_params(("parallel", "arbitrary")),
        name="merge",
    )(ya, yb, yc, wa, wb, wc, gates, gates, gates)


def _attn_body(q_ref, k_ref, v_ref, lam_ref, g_ref, o_ref, qt_ref, vt_ref, m_ref, l_ref, acc_ref,
               *, tk, lam_init):
    seq = k_ref.shape[0]
    tq = q_ref.shape[0]
    nkv = seq // tk

    @pl.when(pl.program_id(2) == 0)
    def _():
        def transpose_values(i, carry):
            r = pl.ds(pl.multiple_of(i * tk, tk), tk)
            vt_ref[i] = v_ref[r, :].astype(F32).T.astype(BF16)
            return carry

        lax.fori_loop(0, nkv, transpose_values, 0)

    qt_ref[...] = q_ref[...].astype(F32).T.astype(BF16)
    m_ref[...] = jnp.full_like(m_ref, -jnp.inf)
    l_ref[...] = jnp.zeros_like(l_ref)
    acc_ref[...] = jnp.zeros_like(acc_ref)

    def step(i, carry):
        rows = pl.ds(pl.multiple_of(i * tk, tk), tk)
        vt = vt_ref[i]
        for c in range(2):
            kc = k_ref[rows, c * DA_DH:(c + 1) * DA_DH]
            for j in range(tq // ATT_STRIP):
                cols = slice(j * ATT_STRIP, (j + 1) * ATT_STRIP)
                st = _dot(kc, qt_ref[c * DA_DH:(c + 1) * DA_DH, cols])
                m_old = m_ref[c, :, cols]
                m_new = jnp.maximum(m_old, jnp.max(st, axis=0, keepdims=True))
                alpha = jnp.exp2(m_old - m_new)
                p = jnp.exp2(st - m_new)
                l_ref[c, :, cols] = alpha * l_ref[c, :, cols] + jnp.sum(p, axis=0, keepdims=True)
                acc_ref[c, :, cols] = alpha * acc_ref[c, :, cols] + _dot(vt, p.astype(BF16))
                m_ref[c, :, cols] = m_new
        return carry

    lax.fori_loop(0, nkv, step, 0, unroll=16)

    lp = lam_ref[...]
    lam = (jnp.exp(jnp.sum(lp[0:1] * lp[1:2], axis=-1, keepdims=True))
           - jnp.exp(jnp.sum(lp[2:3] * lp[3:4], axis=-1, keepdims=True)) + lam_init)
    o = acc_ref[0] / l_ref[0] - lam * (acc_ref[1] / l_ref[1])
    ms = jnp.mean(o * o, axis=0, keepdims=True)
    o = o * lax.rsqrt(ms + EPS) * g_ref[...] * (1.0 - lam_init)
    o_ref[...] = o.T.astype(o_ref.dtype)


def _attention(qk3, rest3, lam_params, norm_g_col, layer, *, tq=1024, tk=256):
    bsz, seq, _ = qk3.shape
    tq = min(tq, seq)
    tk = min(tk, seq)
    hw = 2 * DA_DH
    lam_init = 0.8 - 0.6 * math.exp(-0.3 * layer)
    return pl.pallas_call(
        functools.partial(_attn_body, tk=tk, lam_init=lam_init),
        out_shape=jax.ShapeDtypeStruct((bsz, seq, DA_V_WIDTH), BF16),
        grid=(bsz, DA_HEADS, seq // tq),
        in_specs=[pl.BlockSpec((None, tq, hw), lambda b, h, i: (b, i, QK_Q // hw + h)),
                  pl.BlockSpec((None, seq, hw), lambda b, h, i: (b, 0, QK_K // hw + h)),
                  pl.BlockSpec((None, seq, hw), lambda b, h, i: (b, 0, REST_BV // hw + h)),
                  pl.BlockSpec((4, DA_DH), lambda b, h, i: (0, 0)),
                  pl.BlockSpec((hw, 1), lambda b, h, i: (0, 0))],
        out_specs=pl.BlockSpec((None, tq, hw), lambda b, h, i: (b, i, h)),
        scratch_shapes=[pltpu.VMEM((hw, tq), BF16),
                        pltpu.VMEM((seq // tk, hw, tk), BF16),
                        pltpu.VMEM((2, 1, tq), F32),
                        pltpu.VMEM((2, 1, tq), F32),
                        pltpu.VMEM((2, hw, tq), F32)],
        compiler_params=_params(("parallel", "parallel", "arbitrary")),
        name="diff_attention",
    )(qk3, qk3, rest3, lam_params, norm_g_col)


def _log2(x):
    return jnp.log(x) * LOG2E


def _gla_gates(x, lb):
    xs = x * LOG2E
    e = jnp.exp2(-jnp.abs(xs))
    inv = 1.0 / (1.0 + e)
    sig_neg = jnp.where(xs >= 0.0, e, 1.0) * inv
    ls = jnp.minimum(xs, 0.0) - _log2(1.0 + e)
    if lb is None:
        return sig_neg, ls
    a = _log2(lb)
    b = _log2(1.0 - lb) + ls
    mx = jnp.maximum(a, b)
    log_f = mx + _log2(jnp.exp2(a - mx) + jnp.exp2(b - mx))
    return (1.0 - lb) * sig_neg, log_f


def _gla_chunk(q, x, v, state, lb, consts, reverse):
    tri, sel, blk_eq, row = consts
    cs, sub = GLA_CHUNK, GLA_SUB
    nsub = cs // sub
    k, g = _gla_gates(x, lb)
    b = jnp.dot(tri, g, precision=lax.Precision.HIGHEST, preferred_element_type=F32)
    edge = b[0:1] if reverse else b[cs - 1:cs]
    vb = v.astype(BF16)

    o = _dot_nt((q * jnp.exp2(b)).astype(BF16), state.astype(BF16))
    k_edge = (k * jnp.exp2(edge - b)).astype(BF16)
    new_state = jnp.exp2(edge) * state + _dot_tn(vb, k_edge)

    qs, ks = [], []
    for j in range(nsub):
        if (reverse and j == 0) or (not reverse and j == nsub - 1):
            continue
        lo, hi = j * sub, (j + 1) * sub
        anchor = b[lo:lo + 1] if reverse else b[hi - 1:hi]
        q_ok = (row < lo) if reverse else (row >= hi)
        k_ok = jnp.logical_and(row >= lo, row < hi)
        qs.append(jnp.where(q_ok, q * jnp.exp2(jnp.minimum(b - anchor, 0.0)), 0.0).astype(BF16))
        ks.append(jnp.where(k_ok, k * jnp.exp2(jnp.minimum(anchor - b, 0.0)), 0.0).astype(BF16))
    a_mat = _dot_nt(jnp.concatenate(qs, axis=1), jnp.concatenate(ks, axis=1))

    sub_row = lax.broadcasted_iota(jnp.int32, (sub, 1), 0)
    cols = []
    for s in range(sub):
        keep = (sub_row <= s) if reverse else (sub_row >= s)
        parts = []
        for j in range(nsub):
            lo = j * sub
            bb = b[lo:lo + sub]
            w = jnp.exp2(jnp.minimum(bb - b[lo + s:lo + s + 1], 0.0))
            parts.append(jnp.where(keep, q[lo:lo + sub] * k[lo + s:lo + s + 1] * w, 0.0).astype(BF16))
        cols.append(jnp.concatenate(parts, axis=0))
    diag = _dot(jnp.concatenate(cols, axis=1), sel)
    a_mat = a_mat + jnp.where(blk_eq, diag, 0.0)

    o = o + _dot(a_mat.astype(BF16), vb)
    return o, new_state


def _gla_body(q_ref, ff_ref, fb_ref, i_ref, g_ref, lbraw_ref, ng_ref, o_ref,
              of_ref, ob_ref, st_ref, *, layer):
    seq = q_ref.shape[0]
    cs, sub = GLA_CHUNK, GLA_SUB
    nc = seq // cs

    if layer == 0:
        lbs = (None, None)
    else:
        raw = lbraw_ref[...]
        e = jnp.exp(raw - jnp.max(raw, axis=0, keepdims=True))
        p = e / jnp.sum(e, axis=0, keepdims=True)
        lb2 = jnp.sum(p[1:layer + 1], axis=0)
        lbs = (lb2[0:1], lb2[1:2])

    r_i = lax.broadcasted_iota(jnp.int32, (cs, cs), 0)
    c_i = lax.broadcasted_iota(jnp.int32, (cs, cs), 1)
    tri_f = (c_i <= r_i).astype(F32)
    tri_b = (c_i >= r_i).astype(F32)
    blk_eq = (r_i // sub) == (c_i // sub)
    sr = lax.broadcasted_iota(jnp.int32, (sub * HG_DK, cs), 0)
    sc = lax.broadcasted_iota(jnp.int32, (sub * HG_DK, cs), 1)
    sel = ((sr // HG_DK) == (sc % sub)).astype(BF16)
    row = lax.broadcasted_iota(jnp.int32, (cs, 1), 0)

    st_ref[...] = jnp.zeros_like(st_ref)

    def step(i, carry):
        rf = pl.ds(pl.multiple_of(i * cs, cs), cs)
        rb = pl.ds(pl.multiple_of((nc - 1 - i) * cs, cs), cs)
        o_f, s_f = _gla_chunk(q_ref[rf, :].astype(F32), ff_ref[rf, :].astype(F32), i_ref[rf, :].astype(F32),
                              st_ref[0], lbs[0], (tri_f, sel, blk_eq, row), False)
        of_ref[rf, :] = o_f
        st_ref[0] = s_f
        o_b, s_b = _gla_chunk(q_ref[rb, :].astype(F32), fb_ref[rb, :].astype(F32), i_ref[rb, :].astype(F32),
                              st_ref[1], lbs[1], (tri_b, sel, blk_eq, row), True)
        ob_ref[rb, :] = o_b
        st_ref[1] = s_b
        return carry

    lax.fori_loop(0, nc, step, 0, unroll=GLA_UNROLL)

    tr = min(512, seq)
    ng = ng_ref[...]

    def fin(i, carry):
        r = pl.ds(pl.multiple_of(i * tr, tr), tr)
        o = of_ref[r, :] + ob_ref[r, :]
        ms = jnp.mean(o * o, axis=-1, keepdims=True)
        y = o * lax.rsqrt(ms + EPS) * ng
        gate = g_ref[r, :].astype(F32)
        o_ref[r, :] = (y * gate * _sigmoid(gate)).astype(o_ref.dtype)
        return carry

    lax.fori_loop(0, seq // tr, fin, 0)


def _hgrn2(proj3, lb_raw, norm_g, layer):
    bsz, seq, _ = proj3.shape

    def col(base):
        return pl.BlockSpec((None, seq, HG_DK), lambda b, h: (b, 0, base // HG_DK + h))

    return pl.pallas_call(
        functools.partial(_gla_body, layer=layer),
        out_shape=jax.ShapeDtypeStruct((bsz, seq, HG_WIDTH), BF16),
        grid=(bsz, HG_HEADS),
        in_specs=[col(COL_AQ), col(COL_AFF), col(COL_AFB), col(COL_AI), col(COL_AG),
                  pl.BlockSpec((DEPTH, 2, HG_DK), lambda b, h: (0, 0, h)),
                  pl.BlockSpec((1, HG_DK), lambda b, h: (0, h))],
        out_specs=pl.BlockSpec((None, seq, HG_DK), lambda b, h: (b, 0, h)),
        scratch_shapes=[pltpu.VMEM((seq, HG_DK), F32),
                        pltpu.VMEM((seq, HG_DK), F32),
                        pltpu.VMEM((2, HG_DK, HG_DK), F32)],
        compiler_params=_params(("parallel", "parallel")),
        name="hgrn2",
    )(proj3, proj3, proj3, proj3, proj3, lb_raw, norm_g)


def _gelu(x):
    return 0.5 * x * (1.0 + jnp.tanh(math.sqrt(2.0 / math.pi) * (x + 0.044715 * (x * x * x))))


def _sgu_body(u_ref, v_ref, ng_ref, nb_ref, ws_ref, bs_ref, o_ref):
    tm = u_ref.shape[0]
    v = _gelu(v_ref[...].astype(F32))
    mu = jnp.mean(v, axis=-1, keepdims=True)
    d = v - mu
    var = jnp.mean(d * d, axis=-1, keepdims=True)
    vn = (d * lax.rsqrt(var + EPS) * ng_ref[...] + nb_ref[...]).astype(BF16)
    bs = bs_ref[...]
    for c in range(tm // SG_CHUNK):
        r0 = c * SG_CHUNK
        for g in range(SG_GROUPS):
            c0 = g * SG_GDIM
            mixed = _dot(ws_ref[g], vn[r0:r0 + SG_CHUNK, c0:c0 + SG_GDIM]) + bs[:, g:g + 1]
            u = _gelu(u_ref[r0:r0 + SG_CHUNK, c0:c0 + SG_GDIM].astype(F32))
            o_ref[r0:r0 + SG_CHUNK, c0:c0 + SG_GDIM] = (u * mixed).astype(o_ref.dtype)


def _spatial_gating(proj, norm_g, norm_b, w_s, b_s_t, *, tm=256):
    m = proj.shape[0]
    return pl.pallas_call(
        _sgu_body,
        out_shape=jax.ShapeDtypeStruct((m, SG_WIDTH), BF16),
        grid=(m // tm,),
        in_specs=[pl.BlockSpec((tm, SG_WIDTH), lambda i: (i, REST_CU // SG_WIDTH)),
                  pl.BlockSpec((tm, SG_WIDTH), lambda i: (i, REST_CV // SG_WIDTH)),
                  pl.BlockSpec((1, SG_WIDTH), lambda i: (0, 0)),
                  pl.BlockSpec((1, SG_WIDTH), lambda i: (0, 0)),
                  pl.BlockSpec((SG_GROUPS, SG_CHUNK, SG_CHUNK), lambda i: (0, 0, 0)),
                  pl.BlockSpec((SG_CHUNK, SG_GROUPS), lambda i: (0, 0))],
        out_specs=pl.BlockSpec((tm, SG_WIDTH), lambda i: (i, 0)),
        compiler_params=_params(("parallel",)),
        name="spatial_gating",
    )(proj, proj, norm_g, norm_b, w_s, b_s_t)


def _rotary_tables(seq):
    half = ROT_DIM // 2
    pos = jnp.arange(seq, dtype=F32)
    freqs = ROPE_THETA ** (-jnp.arange(0, ROT_DIM, 2, dtype=F32) / ROT_DIM)
    ang = pos[:, None] * freqs[None, :]
    cos, sin = jnp.cos(ang), jnp.sin(ang)
    rest = LANES - ROT_DIM
    c = jnp.concatenate([cos, cos, jnp.ones((seq, rest), F32)], axis=1)
    s1 = jnp.concatenate([jnp.zeros((seq, half), F32), sin, jnp.zeros((seq, rest), F32)], axis=1)
    s2 = jnp.concatenate([-sin, jnp.zeros((seq, LANES - half), F32)], axis=1)
    q_scale = LOG2E * DA_DH ** -0.5
    return tuple(jnp.stack([t * q_scale, t]) for t in (c, s1, s2))


def kernel(x, w_in, hg_lb_raw, hg_norm_g, da_lambda, da_norm_g, sg_norm_g, sg_norm_b, sg_w_s, sg_b_s, w_branch_a, w_branch_b, w_branch_c, w_gate, b_gate, w_out, ln1_g, ln1_b, w_up, w_down, ln2_g, ln2_b):
    bsz, seq, d = x.shape
    m = bsz * seq
    rot_c, rot_s1, rot_s2 = _rotary_tables(seq)
    xf = x.reshape(m, d)
    xb = xf.astype(BF16)
    wa_b, wb_b, wc_b = (w.astype(BF16) for w in (w_branch_a, w_branch_b, w_branch_c))
    w_out_b = w_out.astype(BF16)
    w_down_b = w_down.astype(BF16)
    for layer in range(DEPTH):
        qk = _proj_qk(xb, w_in, layer, rot_c, rot_s1, rot_s2, seq)
        rest = _proj_rest(xb, w_in, layer)
        gates = _gate_proj(xb, w_gate, layer, b_gate[layer].reshape(1, -1))
        qk3 = qk.reshape(bsz, seq, 2 * DA_QK_WIDTH)
        rest3 = rest.reshape(bsz, seq, REST_WIDTH)
        y_a = _hgrn2(rest3, hg_lb_raw, hg_norm_g[layer].reshape(1, -1), layer)
        y_b = _attention(qk3, rest3, da_lambda[layer], da_norm_g[layer].reshape(-1, 1), layer)
        y_c = _spatial_gating(rest, sg_norm_g[layer].reshape(1, -1), sg_norm_b[layer].reshape(1, -1),
                              sg_w_s[layer].astype(BF16), sg_b_s[layer].T)
        merged = _merge(y_a.reshape(m, HG_WIDTH), y_b.reshape(m, DA_V_WIDTH), y_c,
                        wa_b, wb_b, wc_b, layer, gates)
        xf, xb = _mm_ln(merged, w_out_b, layer, xf, ln1_g[layer].reshape(1, -1),
                        ln1_b[layer].reshape(1, -1), name="out_proj_ln")
        hid = _up_proj(xb, w_up, layer)
        xf, xb = _mm_ln(hid, w_down_b, layer, xf, ln2_g[layer].reshape(1, -1),
                        ln2_b[layer].reshape(1, -1), name="down_proj_ln")
    return xf.reshape(bsz, seq, d)
```

```python
import functools
import math

import jax
import jax.numpy as jnp
from jax import lax
from jax.experimental import pallas as pl
from jax.experimental.pallas import tpu as pltpu

D_MODEL = 4096
DEPTH = 2
HG_WIDTH = D_MODEL // 4
HG_DK = 128
HG_HEADS = HG_WIDTH // HG_DK
DA_DH = 128
DA_HEADS = D_MODEL // (4 * DA_DH)
DA_QK_WIDTH = DA_HEADS * 2 * DA_DH
DA_V_WIDTH = DA_HEADS * 2 * DA_DH
ROPE_THETA = 500000.0
ROT_DIM = DA_DH // 4
SG_CHUNK = 128
SG_WIDTH = D_MODEL // 4
SG_GDIM = 128
SG_GROUPS = SG_WIDTH // SG_GDIM
N_BRANCH = 3
D_FF = 4 * D_MODEL
ALPHA = (2.0 * DEPTH) ** 0.25
EPS = 1e-5
D_IN = 5 * HG_WIDTH + 2 * DA_QK_WIDTH + DA_V_WIDTH + 2 * SG_WIDTH

COL_AQ, COL_AFF, COL_AFB, COL_AI, COL_AG = (i * HG_WIDTH for i in range(5))
COL_BQ = 5 * HG_WIDTH
COL_BK = COL_BQ + DA_QK_WIDTH
COL_BV = COL_BK + DA_QK_WIDTH
COL_CU = COL_BV + DA_V_WIDTH
COL_CV = COL_CU + SG_WIDTH

QK_Q, QK_K = 0, DA_QK_WIDTH
REST_WIDTH = D_IN - 2 * DA_QK_WIDTH
REST_BV = COL_BV - 2 * DA_QK_WIDTH
REST_CU = COL_CU - 2 * DA_QK_WIDTH
REST_CV = COL_CV - 2 * DA_QK_WIDTH

LANES = 128
VMEM_LIMIT = 56 * 1024 * 1024
GLA_CHUNK = 64
GLA_SUB = 16
LN_ROWS = 64
ATT_STRIP = 256
GLA_GROUP = 4
LOG2E = math.log2(math.e)

F32 = jnp.float32
BF16 = jnp.bfloat16


def _params(sem):
    return pltpu.CompilerParams(dimension_semantics=sem, vmem_limit_bytes=VMEM_LIMIT)


def _dot(a, b):
    return jnp.dot(a, b, preferred_element_type=F32)


def _dot_nt(a, b):
    return lax.dot_general(a, b, (((1,), (1,)), ((), ())), preferred_element_type=F32)


def _dot_tn(a, b):
    return lax.dot_general(a, b, (((0,), (0,)), ((), ())), preferred_element_type=F32)


def _sigmoid(x):
    return 0.5 * jnp.tanh(0.5 * x) + 0.5


def _proj_qk_body(x_ref, w_ref, c_ref, s1_ref, s2_ref, o_ref):
    acc = _dot(x_ref[...], w_ref[...].astype(BF16))
    c = c_ref[...]
    s1 = s1_ref[...]
    s2 = s2_ref[...]
    for g in range(o_ref.shape[1] // LANES):
        t = acc[:, g * LANES:(g + 1) * LANES]
        r = (t * c + pltpu.roll(t, ROT_DIM // 2, 1) * s1
             + pltpu.roll(t, LANES - ROT_DIM // 2, 1) * s2)
        o_ref[:, g * LANES:(g + 1) * LANES] = r.astype(o_ref.dtype)


def _proj_qk(h, w, layer, rot_c, rot_s1, rot_s2, seq, *, tm=1024, tn=512):
    m, k = h.shape
    tm = min(tm, seq)
    nseq = seq // tm
    nq = DA_QK_WIDTH // tn
    first = COL_BQ // tn
    tab = pl.BlockSpec((None, tm, LANES), lambda i, j: (j // nq, i % nseq, 0))
    return pl.pallas_call(
        _proj_qk_body,
        out_shape=jax.ShapeDtypeStruct((m, 2 * DA_QK_WIDTH), BF16),
        grid=(m // tm, 2 * nq),
        in_specs=[pl.BlockSpec((tm, k), lambda i, j: (i, 0)),
                  pl.BlockSpec((None, k, tn), lambda i, j: (layer, 0, first + j)),
                  tab, tab, tab],
        out_specs=pl.BlockSpec((tm, tn), lambda i, j: (i, j)),
        compiler_params=_params(("parallel", "arbitrary")),
        name="proj_qk",
    )(h, w, rot_c, rot_s1, rot_s2)


def _proj_rest_body(x_ref, w_ref, o_ref):
    o_ref[...] = _dot(x_ref[...], w_ref[...].astype(BF16)).astype(o_ref.dtype)


def _proj_rest(h, w, layer, *, tm=1024, tn=512):
    m, k = h.shape
    tm = min(tm, m)
    first_skip = COL_BQ // tn
    skipped = 2 * DA_QK_WIDTH // tn
    return pl.pallas_call(
        _proj_rest_body,
        out_shape=jax.ShapeDtypeStruct((m, REST_WIDTH), BF16),
        grid=(m // tm, REST_WIDTH // tn),
        in_specs=[pl.BlockSpec((tm, k), lambda i, j: (i, 0)),
                  pl.BlockSpec((None, k, tn),
                               lambda i, j: (layer, 0, j + jnp.where(j >= first_skip, skipped, 0)))],
        out_specs=pl.BlockSpec((tm, tn), lambda i, j: (i, j)),
        compiler_params=_params(("parallel", "arbitrary")),
        name="proj_rest",
    )(h, w)


def _gate_body(x_ref, w_ref, b_ref, o_ref):
    acc = _dot(x_ref[...], w_ref[...].astype(BF16)) + b_ref[...]
    o_ref[...] = _sigmoid(acc).astype(o_ref.dtype)


def _gate_proj(h, w, layer, b, *, tm=1024, tn=512):
    m, k = h.shape
    n = w.shape[2]
    tm = min(tm, m)
    return pl.pallas_call(
        _gate_body,
        out_shape=jax.ShapeDtypeStruct((m, n), BF16),
        grid=(m // tm, n // tn),
        in_specs=[pl.BlockSpec((tm, k), lambda i, j: (i, 0)),
                  pl.BlockSpec((None, k, tn), lambda i, j: (layer, 0, j)),
                  pl.BlockSpec((1, tn), lambda i, j: (0, j))],
        out_specs=pl.BlockSpec((tm, tn), lambda i, j: (i, j)),
        compiler_params=_params(("parallel", "arbitrary")),
        name="gate_proj",
    )(h, w, b)


def _up_body(x_ref, w_ref, o_ref):
    r = jnp.maximum(_dot(x_ref[...], w_ref[...].astype(BF16)), 0.0)
    o_ref[...] = (r * r).astype(o_ref.dtype)


def _up_proj(h, w, layer, *, tm=1024, tn=512):
    m, k = h.shape
    n = w.shape[2]
    tm = min(tm, m)
    return pl.pallas_call(
        _up_body,
        out_shape=jax.ShapeDtypeStruct((m, n), BF16),
        grid=(m // tm, n // tn),
        in_specs=[pl.BlockSpec((tm, k), lambda i, j: (i, 0)),
                  pl.BlockSpec((None, k, tn), lambda i, j: (layer, 0, j))],
        out_specs=pl.BlockSpec((tm, tn), lambda i, j: (i, j)),
        compiler_params=_params(("parallel", "arbitrary")),
        name="up_proj",
    )(h, w)


def _mm_ln_body(y_ref, w_ref, res_ref, g_ref, b_ref, of_ref, ob_ref, acc_a, acc_b, *, nk, nt):
    i = pl.program_id(0)
    kk = pl.program_id(1)
    slab = res_ref.shape[0]
    rows = min(slab, LN_ROWS)

    def normalise(prev_ref):
        g = g_ref[...]
        b = b_ref[...]
        for r0 in range(0, slab, rows):
            rs = pl.ds(pl.multiple_of(kk * slab + r0, rows), rows)
            t = prev_ref[rs, :] + ALPHA * res_ref[r0:r0 + rows, :]
            prev_ref[rs, :] = jnp.zeros((rows, prev_ref.shape[1]), F32)
            mu = jnp.mean(t, axis=-1, keepdims=True)
            d = t - mu
            var = jnp.mean(d * d, axis=-1, keepdims=True)
            out = d * lax.rsqrt(var + EPS) * g + b
            of_ref[r0:r0 + rows, :] = out
            ob_ref[r0:r0 + rows, :] = out.astype(ob_ref.dtype)

    @pl.when(jnp.logical_and(i == 0, kk == 0))
    def _():
        acc_a[...] = jnp.zeros_like(acc_a)
        acc_b[...] = jnp.zeros_like(acc_b)

    for parity, (cur_ref, prev_ref) in enumerate(((acc_a, acc_b), (acc_b, acc_a))):
        @pl.when(jnp.logical_and(i < nt, i % 2 == parity))
        def _(cur_ref=cur_ref, prev_ref=prev_ref):
            cur_ref[...] += _dot(y_ref[...], w_ref[...])
            normalise(prev_ref)

    @pl.when(i == nt)
    def _():
        normalise(acc_a if (nt - 1) % 2 == 0 else acc_b)


def _mm_ln(y, w, layer, res, g, b, *, tm=512, tk=1024, name):
    m, k = y.shape
    n = w.shape[2]
    tm = min(tm, m)
    nk = k // tk
    nt = m // tm
    slab = tm // nk

    def lagged(i, kk):
        return (jnp.where(i == 0, 0, (i - 1) * nk + kk), 0)

    def last_k(i, kk):
        return jnp.where(i < nt, kk, nk - 1)

    return pl.pallas_call(
        functools.partial(_mm_ln_body, nk=nk, nt=nt),
        out_shape=(jax.ShapeDtypeStruct((m, n), F32), jax.ShapeDtypeStruct((m, n), BF16)),
        grid=(nt + 1, nk),
        in_specs=[pl.BlockSpec((tm, tk), lambda i, kk: (jnp.minimum(i, nt - 1), last_k(i, kk))),
                  pl.BlockSpec((None, tk, n), lambda i, kk: (layer, last_k(i, kk), 0)),
                  pl.BlockSpec((slab, n), lagged),
                  pl.BlockSpec((1, n), lambda i, kk: (0, 0)),
                  pl.BlockSpec((1, n), lambda i, kk: (0, 0))],
        out_specs=(pl.BlockSpec((slab, n), lagged),
                   pl.BlockSpec((slab, n), lagged)),
        scratch_shapes=[pltpu.VMEM((tm, n), F32), pltpu.VMEM((tm, n), F32)],
        compiler_params=_params(("arbitrary", "arbitrary")),
        name=name,
    )(y, w, res, g, b)


def _merge_body(ya_ref, yb_ref, yc_ref, wa_ref, wb_ref, wc_ref, ga_ref, gb_ref, gc_ref, o_ref):
    acc = ga_ref[...].astype(F32) * _dot(ya_ref[...], wa_ref[...])
    acc += gb_ref[...].astype(F32) * _dot(yb_ref[...], wb_ref[...])
    acc += gc_ref[...].astype(F32) * _dot(yc_ref[...], wc_ref[...])
    o_ref[...] = acc.astype(o_ref.dtype)


def _merge(ya, yb, yc, wa, wb, wc, layer, gates, *, tm=1024, tn=512):
    m = ya.shape[0]
    n = wa.shape[2]
    tm = min(tm, m)
    nb = n // tn

    def act(width):
        return pl.BlockSpec((tm, width), lambda i, j: (i, 0))

    def wt(width):
        return pl.BlockSpec((None, width, tn), lambda i, j: (layer, 0, j))

    def gate(branch):
        return pl.BlockSpec((tm, tn), lambda i, j: (i, branch * nb + j))

    return pl.pallas_call(
        _merge_body,
        out_shape=jax.ShapeDtypeStruct((m, n), BF16),
        grid=(m // tm, nb),
        in_specs=[act(ya.shape[1]), act(yb.shape[1]), act(yc.shape[1]),
                  wt(wa.shape[1]), wt(wb.shape[1]), wt(wc.shape[1]),
                  gate(0), gate(1), gate(2)],
        out_specs=pl.BlockSpec((tm, tn), lambda i, j: (i, j)),
        compiler_params=_params(("parallel", "arbitrary")),
        name="merge",
    )(ya, yb, yc, wa, wb, wc, gates, gates, gates)


def _attn_body(q_ref, k_ref, v_ref, lam_ref, g_ref, o_ref, qt_ref, vt_ref, m_ref, l_ref, acc_ref,
               *, tk, lam_init):
    seq = k_ref.shape[0]
    tq = q_ref.shape[0]
    nkv = seq // tk

    @pl.when(pl.program_id(2) == 0)
    def _():
        def transpose_values(i, carry):
            r = pl.ds(pl.multiple_of(i * tk, tk), tk)
            vt_ref[i] = v_ref[r, :].astype(F32).T.astype(BF16)
            return carry

        lax.fori_loop(0, nkv, transpose_values, 0)

    qt_ref[...] = q_ref[...].astype(F32).T.astype(BF16)
    m_ref[...] = jnp.full_like(m_ref, -jnp.inf)
    l_ref[...] = jnp.zeros_like(l_ref)
    acc_ref[...] = jnp.zeros_like(acc_ref)

    def step(i, carry):
        rows = pl.ds(pl.multiple_of(i * tk, tk), tk)
        vt = vt_ref[i]
        for c in range(2):
            kc = k_ref[rows, c * DA_DH:(c + 1) * DA_DH]
            for j in range(tq // ATT_STRIP):
                cols = slice(j * ATT_STRIP, (j + 1) * ATT_STRIP)
                st = _dot(kc, qt_ref[c * DA_DH:(c + 1) * DA_DH, cols])
                m_old = m_ref[c, :, cols]
                m_new = jnp.maximum(m_old, jnp.max(st, axis=0, keepdims=True))
                alpha = jnp.exp2(m_old - m_new)
                p = jnp.exp2(st - m_new)
                l_ref[c, :, cols] = alpha * l_ref[c, :, cols] + jnp.sum(p, axis=0, keepdims=True)
                acc_ref[c, :, cols] = alpha * acc_ref[c, :, cols] + _dot(vt, p.astype(BF16))
                m_ref[c, :, cols] = m_new
        return carry

    lax.fori_loop(0, nkv, step, 0, unroll=16)

    lp = lam_ref[...]
    lam = (jnp.exp(jnp.sum(lp[0:1] * lp[1:2], axis=-1, keepdims=True))
           - jnp.exp(jnp.sum(lp[2:3] * lp[3:4], axis=-1, keepdims=True)) + lam_init)
    o = acc_ref[0] / l_ref[0] - lam * (acc_ref[1] / l_ref[1])
    ms = jnp.mean(o * o, axis=0, keepdims=True)
    o = o * lax.rsqrt(ms + EPS) * g_ref[...] * (1.0 - lam_init)
    o_ref[...] = o.T.astype(o_ref.dtype)


def _attention(qk3, rest3, lam_params, norm_g_col, layer, *, tq=1024, tk=256):
    bsz, seq, _ = qk3.shape
    tq = min(tq, seq)
    tk = min(tk, seq)
    hw = 2 * DA_DH
    lam_init = 0.8 - 0.6 * math.exp(-0.3 * layer)
    return pl.pallas_call(
        functools.partial(_attn_body, tk=tk, lam_init=lam_init),
        out_shape=jax.ShapeDtypeStruct((bsz, seq, DA_V_WIDTH), BF16),
        grid=(bsz, DA_HEADS, seq // tq),
        in_specs=[pl.BlockSpec((None, tq, hw), lambda b, h, i: (b, i, QK_Q // hw + h)),
                  pl.BlockSpec((None, seq, hw), lambda b, h, i: (b, 0, QK_K // hw + h)),
                  pl.BlockSpec((None, seq, hw), lambda b, h, i: (b, 0, REST_BV // hw + h)),
                  pl.BlockSpec((4, DA_DH), lambda b, h, i: (0, 0)),
                  pl.BlockSpec((hw, 1), lambda b, h, i: (0, 0))],
        out_specs=pl.BlockSpec((None, tq, hw), lambda b, h, i: (b, i, h)),
        scratch_shapes=[pltpu.VMEM((hw, tq), BF16),
                        pltpu.VMEM((seq // tk, hw, tk), BF16),
                        pltpu.VMEM((2, 1, tq), F32),
                        pltpu.VMEM((2, 1, tq), F32),
                        pltpu.VMEM((2, hw, tq), F32)],
        compiler_params=_params(("parallel", "parallel", "arbitrary")),
        name="diff_attention",
    )(qk3, qk3, rest3, lam_params, norm_g_col)


def _log2(x):
    return jnp.log(x) * LOG2E


def _gla_gates(x, lb):
    xs = x * LOG2E
    e = jnp.exp2(-jnp.abs(xs))
    inv = 1.0 / (1.0 + e)
    sig_neg = jnp.where(xs >= 0.0, e, 1.0) * inv
    ls = jnp.minimum(xs, 0.0) - _log2(1.0 + e)
    if lb is None:
        return sig_neg, ls
    a = _log2(lb)
    b = _log2(1.0 - lb) + ls
    mx = jnp.maximum(a, b)
    log_f = mx + _log2(jnp.exp2(a - mx) + jnp.exp2(b - mx))
    return (1.0 - lb) * sig_neg, log_f


def _gla_chunk(q, x, v, lb, consts, reverse):
    tri, sel, blk_eq, row = consts
    cs, sub = GLA_CHUNK, GLA_SUB
    nsub = cs // sub
    k, g = _gla_gates(x, lb)
    yield None
    b = jnp.dot(tri, g, precision=lax.Precision.HIGHEST, preferred_element_type=F32)
    yield None
    edge = b[0:1] if reverse else b[cs - 1:cs]
    vb = v.astype(BF16)

    q_dec = (q * jnp.exp2(b)).astype(BF16)
    k_edge = (k * jnp.exp2(edge - b)).astype(BF16)
    decay = jnp.exp2(edge)

    qs, ks = [], []
    for j in range(nsub):
        if (reverse and j == 0) or (not reverse and j == nsub - 1):
            continue
        lo, hi = j * sub, (j + 1) * sub
        anchor = b[lo:lo + 1] if reverse else b[hi - 1:hi]
        q_ok = (row < lo) if reverse else (row >= hi)
        k_ok = jnp.logical_and(row >= lo, row < hi)
        qs.append(jnp.where(q_ok, q * jnp.exp2(jnp.minimum(b - anchor, 0.0)), 0.0).astype(BF16))
        ks.append(jnp.where(k_ok, k * jnp.exp2(jnp.minimum(anchor - b, 0.0)), 0.0).astype(BF16))
    yield None

    sub_row = lax.broadcasted_iota(jnp.int32, (sub, 1), 0)
    cols = []
    for s in range(sub):
        keep = (sub_row <= s) if reverse else (sub_row >= s)
        parts = []
        for j in range(nsub):
            lo = j * sub
            bb = b[lo:lo + sub]
            w = jnp.exp2(jnp.minimum(bb - b[lo + s:lo + s + 1], 0.0))
            parts.append(jnp.where(keep, q[lo:lo + sub] * k[lo + s:lo + s + 1] * w, 0.0).astype(BF16))
        cols.append(jnp.concatenate(parts, axis=0))
    yield None

    a_mat = _dot_nt(jnp.concatenate(qs, axis=1), jnp.concatenate(ks, axis=1))
    diag = _dot(jnp.concatenate(cols, axis=1), sel)
    delta = _dot_tn(vb, k_edge)
    yield None

    a_mat = a_mat + jnp.where(blk_eq, diag, 0.0)
    o_intra = _dot(a_mat.astype(BF16), vb)
    yield o_intra, q_dec, decay, delta


def _gla_body(q_ref, ff_ref, fb_ref, i_ref, g_ref, lbraw_ref, ng_ref, o_ref,
              of_ref, ob_ref, st_ref, *, layer):
    seq = q_ref.shape[0]
    cs, sub = GLA_CHUNK, GLA_SUB
    nc = seq // cs

    if layer == 0:
        lbs = (None, None)
    else:
        raw = lbraw_ref[...]
        e = jnp.exp(raw - jnp.max(raw, axis=0, keepdims=True))
        p = e / jnp.sum(e, axis=0, keepdims=True)
        lb2 = jnp.sum(p[1:layer + 1], axis=0)
        lbs = (lb2[0:1], lb2[1:2])

    r_i = lax.broadcasted_iota(jnp.int32, (cs, cs), 0)
    c_i = lax.broadcasted_iota(jnp.int32, (cs, cs), 1)
    tri_f = (c_i <= r_i).astype(F32)
    tri_b = (c_i >= r_i).astype(F32)
    blk_eq = (r_i // sub) == (c_i // sub)
    sr = lax.broadcasted_iota(jnp.int32, (sub * HG_DK, cs), 0)
    sc = lax.broadcasted_iota(jnp.int32, (sub * HG_DK, cs), 1)
    sel = ((sr // HG_DK) == (sc % sub)).astype(BF16)
    row = lax.broadcasted_iota(jnp.int32, (cs, 1), 0)

    st_ref[...] = jnp.zeros_like(st_ref)

    group = min(GLA_GROUP, nc)
    directions = ((ff_ref, lbs[0], (tri_f, sel, blk_eq, row), False, of_ref),
                  (fb_ref, lbs[1], (tri_b, sel, blk_eq, row), True, ob_ref))

    def step(i, carry):
        chains, slots = [], []
        for u in range(group):
            c = i * group + u
            for d, (f_ref, lb, consts, reverse, out_ref) in enumerate(directions):
                r = pl.ds(pl.multiple_of((nc - 1 - c if reverse else c) * cs, cs), cs)
                chains.append(_gla_chunk(q_ref[r, :].astype(F32), f_ref[r, :].astype(F32),
                                         i_ref[r, :].astype(F32), lb, consts, reverse))
                slots.append((d, r, out_ref))
        for results in zip(*chains):
            pass
        states = [st_ref[0], st_ref[1]]
        for (d, r, out_ref), (o_intra, q_dec, decay, delta) in zip(slots, results):
            out_ref[r, :] = o_intra + _dot_nt(q_dec, states[d].astype(BF16))
            states[d] = decay * states[d] + delta
        st_ref[0] = states[0]
        st_ref[1] = states[1]
        return carry

    lax.fori_loop(0, nc // group, step, 0)

    tr = min(512, seq)
    ng = ng_ref[...]

    def fin(i, carry):
        r = pl.ds(pl.multiple_of(i * tr, tr), tr)
        o = of_ref[r, :] + ob_ref[r, :]
        ms = jnp.mean(o * o, axis=-1, keepdims=True)
        y = o * lax.rsqrt(ms + EPS) * ng
        gate = g_ref[r, :].astype(F32)
        o_ref[r, :] = (y * gate * _sigmoid(gate)).astype(o_ref.dtype)
        return carry

    lax.fori_loop(0, seq // tr, fin, 0)


def _hgrn2(proj3, lb_raw, norm_g, layer):
    bsz, seq, _ = proj3.shape

    def col(base):
        return pl.BlockSpec((None, seq, HG_DK), lambda b, h: (b, 0, base // HG_DK + h))

    return pl.pallas_call(
        functools.partial(_gla_body, layer=layer),
        out_shape=jax.ShapeDtypeStruct((bsz, seq, HG_WIDTH), BF16),
        grid=(bsz, HG_HEADS),
        in_specs=[col(COL_AQ), col(COL_AFF), col(COL_AFB), col(COL_AI), col(COL_AG),
                  pl.BlockSpec((DEPTH, 2, HG_DK), lambda b, h: (0, 0, h)),
                  pl.BlockSpec((1, HG_DK), lambda b, h: (0, h))],
        out_specs=pl.BlockSpec((None, seq, HG_DK), lambda b, h: (b, 0, h)),
        scratch_shapes=[pltpu.VMEM((seq, HG_DK), F32),
                        pltpu.VMEM((seq, HG_DK), F32),
                        pltpu.VMEM((2, HG_DK, HG_DK), F32)],
        compiler_params=_params(("parallel", "parallel")),
        name="hgrn2",
    )(proj3, proj3, proj3, proj3, proj3, lb_raw, norm_g)


def _gelu(x):
    return 0.5 * x * (1.0 + jnp.tanh(math.sqrt(2.0 / math.pi) * (x + 0.044715 * (x * x * x))))


def _sgu_body(u_ref, v_ref, ng_ref, nb_ref, ws_ref, bs_ref, o_ref):
    tm = u_ref.shape[0]
    v = _gelu(v_ref[...].astype(F32))
    mu = jnp.mean(v, axis=-1, keepdims=True)
    d = v - mu
    var = jnp.mean(d * d, axis=-1, keepdims=True)
    vn = (d * lax.rsqrt(var + EPS) * ng_ref[...] + nb_ref[...]).astype(BF16)
    bs = bs_ref[...]
    for c in range(tm // SG_CHUNK):
        r0 = c * SG_CHUNK
        for g in range(SG_GROUPS):
            c0 = g * SG_GDIM
            mixed = _dot(ws_ref[g], vn[r0:r0 + SG_CHUNK, c0:c0 + SG_GDIM]) + bs[:, g:g + 1]
            u = _gelu(u_ref[r0:r0 + SG_CHUNK, c0:c0 + SG_GDIM].astype(F32))
            o_ref[r0:r0 + SG_CHUNK, c0:c0 + SG_GDIM] = (u * mixed).astype(o_ref.dtype)


def _spatial_gating(proj, norm_g, norm_b, w_s, b_s_t, *, tm=256):
    m = proj.shape[0]
    return pl.pallas_call(
        _sgu_body,
        out_shape=jax.ShapeDtypeStruct((m, SG_WIDTH), BF16),
        grid=(m // tm,),
        in_specs=[pl.BlockSpec((tm, SG_WIDTH), lambda i: (i, REST_CU // SG_WIDTH)),
                  pl.BlockSpec((tm, SG_WIDTH), lambda i: (i, REST_CV // SG_WIDTH)),
                  pl.BlockSpec((1, SG_WIDTH), lambda i: (0, 0)),
                  pl.BlockSpec((1, SG_WIDTH), lambda i: (0, 0)),
                  pl.BlockSpec((SG_GROUPS, SG_CHUNK, SG_CHUNK), lambda i: (0, 0, 0)),
                  pl.BlockSpec((SG_CHUNK, SG_GROUPS), lambda i: (0, 0))],
        out_specs=pl.BlockSpec((tm, SG_WIDTH), lambda i: (i, 0)),
        compiler_params=_params(("parallel",)),
        name="spatial_gating",
    )(proj, proj, norm_g, norm_b, w_s, b_s_t)


def _rotary_tables(seq):
    half = ROT_DIM // 2
    pos = jnp.arange(seq, dtype=F32)
    freqs = ROPE_THETA ** (-jnp.arange(0, ROT_DIM, 2, dtype=F32) / ROT_DIM)
    ang = pos[:, None] * freqs[None, :]
    cos, sin = jnp.cos(ang), jnp.sin(ang)
    rest = LANES - ROT_DIM
    c = jnp.concatenate([cos, cos, jnp.ones((seq, rest), F32)], axis=1)
    s1 = jnp.concatenate([jnp.zeros((seq, half), F32), sin, jnp.zeros((seq, rest), F32)], axis=1)
    s2 = jnp.concatenate([-sin, jnp.zeros((seq, LANES - half), F32)], axis=1)
    q_scale = LOG2E * DA_DH ** -0.5
    return tuple(jnp.stack([t * q_scale, t]) for t in (c, s1, s2))


def kernel(x, w_in, hg_lb_raw, hg_norm_g, da_lambda, da_norm_g, sg_norm_g, sg_norm_b, sg_w_s, sg_b_s, w_branch_a, w_branch_b, w_branch_c, w_gate, b_gate, w_out, ln1_g, ln1_b, w_up, w_down, ln2_g, ln2_b):
    bsz, seq, d = x.shape
    m = bsz * seq
    rot_c, rot_s1, rot_s2 = _rotary_tables(seq)
    xf = x.reshape(m, d)
    xb = xf.astype(BF16)
    wa_b, wb_b, wc_b = (w.astype(BF16) for w in (w_branch_a, w_branch_b, w_branch_c))
    w_out_b = w_out.astype(BF16)
    w_down_b = w_down.astype(BF16)
    for layer in range(DEPTH):
        qk = _proj_qk(xb, w_in, layer, rot_c, rot_s1, rot_s2, seq)
        rest = _proj_rest(xb, w_in, layer)
        gates = _gate_proj(xb, w_gate, layer, b_gate[layer].reshape(1, -1))
        qk3 = qk.reshape(bsz, seq, 2 * DA_QK_WIDTH)
        rest3 = rest.reshape(bsz, seq, REST_WIDTH)
        y_a = _hgrn2(rest3, hg_lb_raw, hg_norm_g[layer].reshape(1, -1), layer)
        y_b = _attention(qk3, rest3, da_lambda[layer], da_norm_g[layer].reshape(-1, 1), layer)
        y_c = _spatial_gating(rest, sg_norm_g[layer].reshape(1, -1), sg_norm_b[layer].reshape(1, -1),
                              sg_w_s[layer].astype(BF16), sg_b_s[layer].T)
        merged = _merge(y_a.reshape(m, HG_WIDTH), y_b.reshape(m, DA_V_WIDTH), y_c,
                        wa_b, wb_b, wc_b, layer, gates)
        xf, xb = _mm_ln(merged, w_out_b, layer, xf, ln1_g[layer].reshape(1, -1),
                        ln1_b[layer].reshape(1, -1), name="out_proj_ln")
        hid = _up_proj(xb, w_up, layer)
        xf, xb = _mm_ln(hid, w_down_b, layer, xf, ln2_g[layer].reshape(1, -1),
                        ln2_b[layer].reshape(1, -1), name="down_proj_ln")
    return xf.reshape(bsz, seq, d)
```

```python
import functools
import math

import jax
import jax.numpy as jnp
from jax import lax
from jax.experimental import pallas as pl
from jax.experimental.pallas import tpu as pltpu

D_MODEL = 4096
DEPTH = 2
HG_WIDTH = D_MODEL // 4
HG_DK = 128
HG_HEADS = HG_WIDTH // HG_DK
DA_DH = 128
DA_HEADS = D_MODEL // (4 * DA_DH)
DA_QK_WIDTH = DA_HEADS * 2 * DA_DH
DA_V_WIDTH = DA_HEADS * 2 * DA_DH
ROPE_THETA = 500000.0
ROT_DIM = DA_DH // 4
SG_CHUNK = 128
SG_WIDTH = D_MODEL // 4
SG_GDIM = 128
SG_GROUPS = SG_WIDTH // SG_GDIM
N_BRANCH = 3
D_FF = 4 * D_MODEL
ALPHA = (2.0 * DEPTH) ** 0.25
EPS = 1e-5
D_IN = 5 * HG_WIDTH + 2 * DA_QK_WIDTH + DA_V_WIDTH + 2 * SG_WIDTH

COL_AQ, COL_AFF, COL_AFB, COL_AI, COL_AG = (i * HG_WIDTH for i in range(5))
COL_BQ = 5 * HG_WIDTH
COL_BK = COL_BQ + DA_QK_WIDTH
COL_BV = COL_BK + DA_QK_WIDTH
COL_CU = COL_BV + DA_V_WIDTH
COL_CV = COL_CU + SG_WIDTH

QK_Q, QK_K = 0, DA_QK_WIDTH
REST_WIDTH = D_IN - 2 * DA_QK_WIDTH
REST_BV = COL_BV - 2 * DA_QK_WIDTH
REST_CU = COL_CU - 2 * DA_QK_WIDTH
REST_CV = COL_CV - 2 * DA_QK_WIDTH

LANES = 128
VMEM_LIMIT = 56 * 1024 * 1024
GLA_CHUNK = 64
GLA_SUB = 16
LN_ROWS = 64
ATT_STRIP = 256
GLA_GROUP = 8
LOG2E = math.log2(math.e)
MASKED_EXP = -1e30

F32 = jnp.float32
BF16 = jnp.bfloat16


def _params(sem):
    return pltpu.CompilerParams(dimension_semantics=sem, vmem_limit_bytes=VMEM_LIMIT)


def _dot(a, b):
    return jnp.dot(a, b, preferred_element_type=F32)


def _dot_nt(a, b):
    return lax.dot_general(a, b, (((1,), (1,)), ((), ())), preferred_element_type=F32)


def _dot_tn(a, b):
    return lax.dot_general(a, b, (((0,), (0,)), ((), ())), preferred_element_type=F32)


def _sigmoid(x):
    return 0.5 * jnp.tanh(0.5 * x) + 0.5


def _proj_qk_body(x_ref, w_ref, c_ref, s1_ref, s2_ref, o_ref):
    acc = _dot(x_ref[...], w_ref[...].astype(BF16))
    c = c_ref[...]
    s1 = s1_ref[...]
    s2 = s2_ref[...]
    for g in range(o_ref.shape[1] // LANES):
        t = acc[:, g * LANES:(g + 1) * LANES]
        r = (t * c + pltpu.roll(t, ROT_DIM // 2, 1) * s1
             + pltpu.roll(t, LANES - ROT_DIM // 2, 1) * s2)
        o_ref[:, g * LANES:(g + 1) * LANES] = r.astype(o_ref.dtype)


def _proj_qk(h, w, layer, rot_c, rot_s1, rot_s2, seq, *, tm=1024, tn=512):
    m, k = h.shape
    tm = min(tm, seq)
    nseq = seq // tm
    nq = DA_QK_WIDTH // tn
    first = COL_BQ // tn
    tab = pl.BlockSpec((None, tm, LANES), lambda i, j: (j // nq, i % nseq, 0))
    return pl.pallas_call(
        _proj_qk_body,
        out_shape=jax.ShapeDtypeStruct((m, 2 * DA_QK_WIDTH), BF16),
        grid=(m // tm, 2 * nq),
        in_specs=[pl.BlockSpec((tm, k), lambda i, j: (i, 0)),
                  pl.BlockSpec((None, k, tn), lambda i, j: (layer, 0, first + j)),
                  tab, tab, tab],
        out_specs=pl.BlockSpec((tm, tn), lambda i, j: (i, j)),
        compiler_params=_params(("parallel", "arbitrary")),
        name="proj_qk",
    )(h, w, rot_c, rot_s1, rot_s2)


def _proj_rest_body(x_ref, w_ref, o_ref):
    o_ref[...] = _dot(x_ref[...], w_ref[...].astype(BF16)).astype(o_ref.dtype)


def _proj_rest(h, w, layer, *, tm=1024, tn=512):
    m, k = h.shape
    tm = min(tm, m)
    first_skip = COL_BQ // tn
    skipped = 2 * DA_QK_WIDTH // tn
    return pl.pallas_call(
        _proj_rest_body,
        out_shape=jax.ShapeDtypeStruct((m, REST_WIDTH), BF16),
        grid=(m // tm, REST_WIDTH // tn),
        in_specs=[pl.BlockSpec((tm, k), lambda i, j: (i, 0)),
                  pl.BlockSpec((None, k, tn),
                               lambda i, j: (layer, 0, j + jnp.where(j >= first_skip, skipped, 0)))],
        out_specs=pl.BlockSpec((tm, tn), lambda i, j: (i, j)),
        compiler_params=_params(("parallel", "arbitrary")),
        name="proj_rest",
    )(h, w)


def _gate_body(x_ref, w_ref, b_ref, o_ref):
    acc = _dot(x_ref[...], w_ref[...].astype(BF16)) + b_ref[...]
    o_ref[...] = _sigmoid(acc).astype(o_ref.dtype)


def _gate_proj(h, w, layer, b, *, tm=1024, tn=512):
    m, k = h.shape
    n = w.shape[2]
    tm = min(tm, m)
    return pl.pallas_call(
        _gate_body,
        out_shape=jax.ShapeDtypeStruct((m, n), BF16),
        grid=(m // tm, n // tn),
        in_specs=[pl.BlockSpec((tm, k), lambda i, j: (i, 0)),
                  pl.BlockSpec((None, k, tn), lambda i, j: (layer, 0, j)),
                  pl.BlockSpec((1, tn), lambda i, j: (0, j))],
        out_specs=pl.BlockSpec((tm, tn), lambda i, j: (i, j)),
        compiler_params=_params(("parallel", "arbitrary")),
        name="gate_proj",
    )(h, w, b)


def _up_body(x_ref, w_ref, o_ref):
    r = jnp.maximum(_dot(x_ref[...], w_ref[...].astype(BF16)), 0.0)
    o_ref[...] = (r * r).astype(o_ref.dtype)


def _up_proj(h, w, layer, *, tm=1024, tn=512):
    m, k = h.shape
    n = w.shape[2]
    tm = min(tm, m)
    return pl.pallas_call(
        _up_body,
        out_shape=jax.ShapeDtypeStruct((m, n), BF16),
        grid=(m // tm, n // tn),
        in_specs=[pl.BlockSpec((tm, k), lambda i, j: (i, 0)),
                  pl.BlockSpec((None, k, tn), lambda i, j: (layer, 0, j))],
        out_specs=pl.BlockSpec((tm, tn), lambda i, j: (i, j)),
        compiler_params=_params(("parallel", "arbitrary")),
        name="up_proj",
    )(h, w)


def _mm_ln_body(y_ref, w_ref, res_ref, g_ref, b_ref, of_ref, ob_ref, acc_a, acc_b, *, nk, nt):
    i = pl.program_id(0)
    kk = pl.program_id(1)
    slab = res_ref.shape[0]
    rows = min(slab, LN_ROWS)

    def normalise(prev_ref):
        g = g_ref[...]
        b = b_ref[...]
        for r0 in range(0, slab, rows):
            rs = pl.ds(pl.multiple_of(kk * slab + r0, rows), rows)
            t = prev_ref[rs, :] + ALPHA * res_ref[r0:r0 + rows, :]
            prev_ref[rs, :] = jnp.zeros((rows, prev_ref.shape[1]), F32)
            mu = jnp.mean(t, axis=-1, keepdims=True)
            d = t - mu
            var = jnp.mean(d * d, axis=-1, keepdims=True)
            out = d * lax.rsqrt(var + EPS) * g + b
            of_ref[r0:r0 + rows, :] = out
            ob_ref[r0:r0 + rows, :] = out.astype(ob_ref.dtype)

    @pl.when(jnp.logical_and(i == 0, kk == 0))
    def _():
        acc_a[...] = jnp.zeros_like(acc_a)
        acc_b[...] = jnp.zeros_like(acc_b)

    for parity, (cur_ref, prev_ref) in enumerate(((acc_a, acc_b), (acc_b, acc_a))):
        @pl.when(jnp.logical_and(i < nt, i % 2 == parity))
        def _(cur_ref=cur_ref, prev_ref=prev_ref):
            cur_ref[...] += _dot(y_ref[...], w_ref[...])
            normalise(prev_ref)

    @pl.when(i == nt)
    def _():
        normalise(acc_a if (nt - 1) % 2 == 0 else acc_b)


def _mm_ln(y, w, layer, res, g, b, *, tm=512, tk=1024, name):
    m, k = y.shape
    n = w.shape[2]
    tm = min(tm, m)
    nk = k // tk
    nt = m // tm
    slab = tm // nk

    def lagged(i, kk):
        return (jnp.where(i == 0, 0, (i - 1) * nk + kk), 0)

    def last_k(i, kk):
        return jnp.where(i < nt, kk, nk - 1)

    return pl.pallas_call(
        functools.partial(_mm_ln_body, nk=nk, nt=nt),
        out_shape=(jax.ShapeDtypeStruct((m, n), F32), jax.ShapeDtypeStruct((m, n), BF16)),
        grid=(nt + 1, nk),
        in_specs=[pl.BlockSpec((tm, tk), lambda i, kk: (jnp.minimum(i, nt - 1), last_k(i, kk))),
                  pl.BlockSpec((None, tk, n), lambda i, kk: (layer, last_k(i, kk), 0)),
                  pl.BlockSpec((slab, n), lagged),
                  pl.BlockSpec((1, n), lambda i, kk: (0, 0)),
                  pl.BlockSpec((1, n), lambda i, kk: (0, 0))],
        out_specs=(pl.BlockSpec((slab, n), lagged),
                   pl.BlockSpec((slab, n), lagged)),
        scratch_shapes=[pltpu.VMEM((tm, n), F32), pltpu.VMEM((tm, n), F32)],
        compiler_params=_params(("arbitrary", "arbitrary")),
        name=name,
    )(y, w, res, g, b)


def _merge_body(ya_ref, yb_ref, yc_ref, wa_ref, wb_ref, wc_ref, ga_ref, gb_ref, gc_ref, o_ref):
    acc = ga_ref[...].astype(F32) * _dot(ya_ref[...], wa_ref[...])
    acc += gb_ref[...].astype(F32) * _dot(yb_ref[...], wb_ref[...])
    acc += gc_ref[...].astype(F32) * _dot(yc_ref[...], wc_ref[...])
    o_ref[...] = acc.astype(o_ref.dtype)


def _merge(ya, yb, yc, wa, wb, wc, layer, gates, *, tm=1024, tn=512):
    m = ya.shape[0]
    n = wa.shape[2]
    tm = min(tm, m)
    nb = n // tn

    def act(width):
        return pl.BlockSpec((tm, width), lambda i, j: (i, 0))

    def wt(width):
        return pl.BlockSpec((None, width, tn), lambda i, j: (layer, 0, j))

    def gate(branch):
        return pl.BlockSpec((tm, tn), lambda i, j: (i, branch * nb + j))

    return pl.pallas_call(
        _merge_body,
        out_shape=jax.ShapeDtypeStruct((m, n), BF16),
        grid=(m // tm, nb),
        in_specs=[act(ya.shape[1]), act(yb.shape[1]), act(yc.shape[1]),
                  wt(wa.shape[1]), wt(wb.shape[1]), wt(wc.shape[1]),
                  gate(0), gate(1), gate(2)],
        out_specs=pl.BlockSpec((tm, tn), lambda i, j: (i, j)),
        compiler_params=_params(("parallel", "arbitrary")),
        name="merge",
    )(ya, yb, yc, wa, wb, wc, gates, gates, gates)


def _attn_body(q_ref, k_ref, v_ref, lam_ref, g_ref, o_ref, qt_ref, vt_ref, m_ref, l_ref, acc_ref,
               *, tk, lam_init):
    seq = k_ref.shape[0]
    tq = q_ref.shape[0]
    nkv = seq // tk

    @pl.when(pl.program_id(2) == 0)
    def _():
        def transpose_values(i, carry):
            r = pl.ds(pl.multiple_of(i * tk, tk), tk)
            vt_ref[i] = v_ref[r, :].astype(F32).T.astype(BF16)
            return carry

        lax.fori_loop(0, nkv, transpose_values, 0)

    qt_ref[...] = q_ref[...].astype(F32).T.astype(BF16)
    m_ref[...] = jnp.full_like(m_ref, -jnp.inf)
    l_ref[...] = jnp.zeros_like(l_ref)
    acc_ref[...] = jnp.zeros_like(acc_ref)

    def step(i, carry):
        rows = pl.ds(pl.multiple_of(i * tk, tk), tk)
        vt = vt_ref[i]
        for j in range(tq // ATT_STRIP):
            for c in range(2):
                kc = k_ref[rows, c * DA_DH:(c + 1) * DA_DH]
                cols = slice(j * ATT_STRIP, (j + 1) * ATT_STRIP)
                st = _dot(kc, qt_ref[c * DA_DH:(c + 1) * DA_DH, cols])
                m_old = m_ref[c, :, cols]
                m_new = jnp.maximum(m_old, jnp.max(st, axis=0, keepdims=True))
                alpha = jnp.exp2(m_old - m_new)
                p = jnp.exp2(st - m_new)
                l_ref[c, :, cols] = alpha * l_ref[c, :, cols] + jnp.sum(p, axis=0, keepdims=True)
                acc_ref[c, :, cols] = alpha * acc_ref[c, :, cols] + _dot(vt, p.astype(BF16))
                m_ref[c, :, cols] = m_new
        return carry

    lax.fori_loop(0, nkv, step, 0, unroll=16)

    lp = lam_ref[...]
    lam = (jnp.exp(jnp.sum(lp[0:1] * lp[1:2], axis=-1, keepdims=True))
           - jnp.exp(jnp.sum(lp[2:3] * lp[3:4], axis=-1, keepdims=True)) + lam_init)
    o = acc_ref[0] / l_ref[0] - lam * (acc_ref[1] / l_ref[1])
    ms = jnp.mean(o * o, axis=0, keepdims=True)
    o = o * lax.rsqrt(ms + EPS) * g_ref[...] * (1.0 - lam_init)
    o_ref[...] = o.T.astype(o_ref.dtype)


def _attention(qk3, rest3, lam_params, norm_g_col, layer, *, tq=1024, tk=256):
    bsz, seq, _ = qk3.shape
    tq = min(tq, seq)
    tk = min(tk, seq)
    hw = 2 * DA_DH
    lam_init = 0.8 - 0.6 * math.exp(-0.3 * layer)
    return pl.pallas_call(
        functools.partial(_attn_body, tk=tk, lam_init=lam_init),
        out_shape=jax.ShapeDtypeStruct((bsz, seq, DA_V_WIDTH), BF16),
        grid=(bsz, DA_HEADS, seq // tq),
        in_specs=[pl.BlockSpec((None, tq, hw), lambda b, h, i: (b, i, QK_Q // hw + h)),
                  pl.BlockSpec((None, seq, hw), lambda b, h, i: (b, 0, QK_K // hw + h)),
                  pl.BlockSpec((None, seq, hw), lambda b, h, i: (b, 0, REST_BV // hw + h)),
                  pl.BlockSpec((4, DA_DH), lambda b, h, i: (0, 0)),
                  pl.BlockSpec((hw, 1), lambda b, h, i: (0, 0))],
        out_specs=pl.BlockSpec((None, tq, hw), lambda b, h, i: (b, i, h)),
        scratch_shapes=[pltpu.VMEM((hw, tq), BF16),
                        pltpu.VMEM((seq // tk, hw, tk), BF16),
                        pltpu.VMEM((2, 1, tq), F32),
                        pltpu.VMEM((2, 1, tq), F32),
                        pltpu.VMEM((2, hw, tq), F32)],
        compiler_params=_params(("parallel", "parallel", "arbitrary")),
        name="diff_attention",
    )(qk3, qk3, rest3, lam_params, norm_g_col)


def _log2(x):
    return jnp.log(x) * LOG2E


def _gla_gates(x, lb):
    xs = x * LOG2E
    e = jnp.exp2(-jnp.abs(xs))
    ls = jnp.minimum(xs, 0.0) - _log2(1.0 + e)
    ls_neg = ls - xs
    if lb is None:
        return ls_neg, ls
    a = _log2(lb)
    rest = _log2(1.0 - lb)
    b = rest + ls
    mx = jnp.maximum(a, b)
    log_f = mx + _log2(jnp.exp2(a - mx) + jnp.exp2(b - mx))
    return rest + ls_neg, log_f


def _gla_chunk(q, x, v, lb, consts, reverse):
    tri, sel, blk_eq, row, keeps = consts
    cs, sub = GLA_CHUNK, GLA_SUB
    nsub = cs // sub
    log_k, g = _gla_gates(x, lb)
    yield None
    b = jnp.dot(tri, g, precision=lax.Precision.HIGHEST, preferred_element_type=F32)
    yield None
    edge = b[0:1] if reverse else b[cs - 1:cs]
    vb = v.astype(BF16)
    beta = b - log_k

    q_dec = (q * jnp.exp2(b)).astype(BF16)
    k_edge = jnp.exp2(edge - beta).astype(BF16)
    decay = jnp.exp2(edge)

    qs, ks = [], []
    for j in range(nsub):
        if (reverse and j == 0) or (not reverse and j == nsub - 1):
            continue
        lo, hi = j * sub, (j + 1) * sub
        anchor = b[lo:lo + 1] if reverse else b[hi - 1:hi]
        q_ok = (row < lo) if reverse else (row >= hi)
        k_ok = jnp.logical_and(row >= lo, row < hi)
        qs.append((q * jnp.exp2(jnp.where(q_ok, b - anchor, MASKED_EXP))).astype(BF16))
        ks.append(jnp.exp2(jnp.where(k_ok, anchor - beta, MASKED_EXP)).astype(BF16))
    yield None

    cols = []
    for s in range(sub):
        keep = keeps[s]
        parts = []
        for j in range(nsub):
            lo = j * sub
            bb = b[lo:lo + sub]
            w = jnp.exp2(jnp.where(keep, bb - beta[lo + s:lo + s + 1], MASKED_EXP))
            parts.append((q[lo:lo + sub] * w).astype(BF16))
        cols.append(jnp.concatenate(parts, axis=0))
    yield None

    a_mat = _dot_nt(jnp.concatenate(qs, axis=1), jnp.concatenate(ks, axis=1))
    diag = _dot(jnp.concatenate(cols, axis=1), sel)
    delta = _dot_tn(vb, k_edge)
    yield None

    a_mat = a_mat + jnp.where(blk_eq, diag, 0.0)
    o_intra = _dot(a_mat.astype(BF16), vb)
    yield o_intra, q_dec, decay, delta


def _gla_body(q_ref, ff_ref, fb_ref, i_ref, g_ref, lbraw_ref, ng_ref, o_ref,
              of_ref, ob_ref, st_ref, *, layer):
    seq = q_ref.shape[0]
    cs, sub = GLA_CHUNK, GLA_SUB
    nc = seq // cs

    if layer == 0:
        lbs = (None, None)
    else:
        raw = lbraw_ref[...]
        e = jnp.exp(raw - jnp.max(raw, axis=0, keepdims=True))
        p = e / jnp.sum(e, axis=0, keepdims=True)
        lb2 = jnp.sum(p[1:layer + 1], axis=0)
        lbs = (lb2[0:1], lb2[1:2])

    r_i = lax.broadcasted_iota(jnp.int32, (cs, cs), 0)
    c_i = lax.broadcasted_iota(jnp.int32, (cs, cs), 1)
    tri_f = (c_i <= r_i).astype(F32)
    tri_b = (c_i >= r_i).astype(F32)
    blk_eq = (r_i // sub) == (c_i // sub)
    sr = lax.broadcasted_iota(jnp.int32, (sub * HG_DK, cs), 0)
    sc = lax.broadcasted_iota(jnp.int32, (sub * HG_DK, cs), 1)
    sel = ((sr // HG_DK) == (sc % sub)).astype(BF16)
    row = lax.broadcasted_iota(jnp.int32, (cs, 1), 0)

    st_ref[...] = jnp.zeros_like(st_ref)

    group = min(GLA_GROUP, nc)
    sub_row = lax.broadcasted_iota(jnp.int32, (sub, HG_DK), 0)
    keeps_f = [sub_row >= s for s in range(sub)]
    keeps_b = [sub_row <= s for s in range(sub)]
    directions = ((ff_ref, lbs[0], (tri_f, sel, blk_eq, row, keeps_f), False, of_ref),
                  (fb_ref, lbs[1], (tri_b, sel, blk_eq, row, keeps_b), True, ob_ref))

    def step(i, carry):
        chains, slots = [], []
        for u in range(group):
            c = i * group + u
            for d, (f_ref, lb, consts, reverse, out_ref) in enumerate(directions):
                r = pl.ds(pl.multiple_of((nc - 1 - c if reverse else c) * cs, cs), cs)
                chains.append(_gla_chunk(q_ref[r, :].astype(F32), f_ref[r, :].astype(F32),
                                         i_ref[r, :].astype(F32), lb, consts, reverse))
                slots.append((d, r, out_ref))
        for results in zip(*chains):
            pass
        states = [st_ref[0], st_ref[1]]
        for (d, r, out_ref), (o_intra, q_dec, decay, delta) in zip(slots, results):
            out_ref[r, :] = o_intra + _dot_nt(q_dec, states[d].astype(BF16))
            states[d] = decay * states[d] + delta
        st_ref[0] = states[0]
        st_ref[1] = states[1]
        return carry

    lax.fori_loop(0, nc // group, step, 0)

    tr = min(512, seq)
    ng = ng_ref[...]

    def fin(i, carry):
        r = pl.ds(pl.multiple_of(i * tr, tr), tr)
        o = of_ref[r, :] + ob_ref[r, :]
        ms = jnp.mean(o * o, axis=-1, keepdims=True)
        y = o * lax.rsqrt(ms + EPS) * ng
        gate = g_ref[r, :].astype(F32)
        o_ref[r, :] = (y * gate * _sigmoid(gate)).astype(o_ref.dtype)
        return carry

    lax.fori_loop(0, seq // tr, fin, 0)


def _hgrn2(proj3, lb_raw, norm_g, layer):
    bsz, seq, _ = proj3.shape

    def col(base):
        return pl.BlockSpec((None, seq, HG_DK), lambda b, h: (b, 0, base // HG_DK + h))

    return pl.pallas_call(
        functools.partial(_gla_body, layer=layer),
        out_shape=jax.ShapeDtypeStruct((bsz, seq, HG_WIDTH), BF16),
        grid=(bsz, HG_HEADS),
        in_specs=[col(COL_AQ), col(COL_AFF), col(COL_AFB), col(COL_AI), col(COL_AG),
                  pl.BlockSpec((DEPTH, 2, HG_DK), lambda b, h: (0, 0, h)),
                  pl.BlockSpec((1, HG_DK), lambda b, h: (0, h))],
        out_specs=pl.BlockSpec((None, seq, HG_DK), lambda b, h: (b, 0, h)),
        scratch_shapes=[pltpu.VMEM((seq, HG_DK), F32),
                        pltpu.VMEM((seq, HG_DK), F32),
                        pltpu.VMEM((2, HG_DK, HG_DK), F32)],
        compiler_params=_params(("parallel", "parallel")),
        name="hgrn2",
    )(proj3, proj3, proj3, proj3, proj3, lb_raw, norm_g)


def _gelu(x):
    return 0.5 * x * (1.0 + jnp.tanh(math.sqrt(2.0 / math.pi) * (x + 0.044715 * (x * x * x))))


def _sgu_body(u_ref, v_ref, ng_ref, nb_ref, ws_ref, bs_ref, o_ref):
    tm = u_ref.shape[0]
    v = _gelu(v_ref[...].astype(F32))
    mu = jnp.mean(v, axis=-1, keepdims=True)
    d = v - mu
    var = jnp.mean(d * d, axis=-1, keepdims=True)
    vn = (d * lax.rsqrt(var + EPS) * ng_ref[...] + nb_ref[...]).astype(BF16)
    bs = bs_ref[...]
    for c in range(tm // SG_CHUNK):
        r0 = c * SG_CHUNK
        for g in range(SG_GROUPS):
            c0 = g * SG_GDIM
            mixed = _dot(ws_ref[g], vn[r0:r0 + SG_CHUNK, c0:c0 + SG_GDIM]) + bs[:, g:g + 1]
            u = _gelu(u_ref[r0:r0 + SG_CHUNK, c0:c0 + SG_GDIM].astype(F32))
            o_ref[r0:r0 + SG_CHUNK, c0:c0 + SG_GDIM] = (u * mixed).astype(o_ref.dtype)


def _spatial_gating(proj, norm_g, norm_b, w_s, b_s_t, *, tm=256):
    m = proj.shape[0]
    return pl.pallas_call(
        _sgu_body,
        out_shape=jax.ShapeDtypeStruct((m, SG_WIDTH), BF16),
        grid=(m // tm,),
        in_specs=[pl.BlockSpec((tm, SG_WIDTH), lambda i: (i, REST_CU // SG_WIDTH)),
                  pl.BlockSpec((tm, SG_WIDTH), lambda i: (i, REST_CV // SG_WIDTH)),
                  pl.BlockSpec((1, SG_WIDTH), lambda i: (0, 0)),
                  pl.BlockSpec((1, SG_WIDTH), lambda i: (0, 0)),
                  pl.BlockSpec((SG_GROUPS, SG_CHUNK, SG_CHUNK), lambda i: (0, 0, 0)),
                  pl.BlockSpec((SG_CHUNK, SG_GROUPS), lambda i: (0, 0))],
        out_specs=pl.BlockSpec((tm, SG_WIDTH), lambda i: (i, 0)),
        compiler_params=_params(("parallel",)),
        name="spatial_gating",
    )(proj, proj, norm_g, norm_b, w_s, b_s_t)


def _rotary_tables(seq):
    half = ROT_DIM // 2
    pos = jnp.arange(seq, dtype=F32)
    freqs = ROPE_THETA ** (-jnp.arange(0, ROT_DIM, 2, dtype=F32) / ROT_DIM)
    ang = pos[:, None] * freqs[None, :]
    cos, sin = jnp.cos(ang), jnp.sin(ang)
    rest = LANES - ROT_DIM
    c = jnp.concatenate([cos, cos, jnp.ones((seq, rest), F32)], axis=1)
    s1 = jnp.concatenate([jnp.zeros((seq, half), F32), sin, jnp.zeros((seq, rest), F32)], axis=1)
    s2 = jnp.concatenate([-sin, jnp.zeros((seq, LANES - half), F32)], axis=1)
    q_scale = LOG2E * DA_DH ** -0.5
    return tuple(jnp.stack([t * q_scale, t]) for t in (c, s1, s2))


def kernel(x, w_in, hg_lb_raw, hg_norm_g, da_lambda, da_norm_g, sg_norm_g, sg_norm_b, sg_w_s, sg_b_s, w_branch_a, w_branch_b, w_branch_c, w_gate, b_gate, w_out, ln1_g, ln1_b, w_up, w_down, ln2_g, ln2_b):
    bsz, seq, d = x.shape
    m = bsz * seq
    rot_c, rot_s1, rot_s2 = _rotary_tables(seq)
    xf = x.reshape(m, d)
    xb = xf.astype(BF16)
    wa_b, wb_b, wc_b = (w.astype(BF16) for w in (w_branch_a, w_branch_b, w_branch_c))
    w_out_b = w_out.astype(BF16)
    w_down_b = w_down.astype(BF16)
    for layer in range(DEPTH):
        qk = _proj_qk(xb, w_in, layer, rot_c, rot_s1, rot_s2, seq)
        rest = _proj_rest(xb, w_in, layer)
        gates = _gate_proj(xb, w_gate, layer, b_gate[layer].reshape(1, -1))
        qk3 = qk.reshape(bsz, seq, 2 * DA_QK_WIDTH)
        rest3 = rest.reshape(bsz, seq, REST_WIDTH)
        y_a = _hgrn2(rest3, hg_lb_raw, hg_norm_g[layer].reshape(1, -1), layer)
        y_b = _attention(qk3, rest3, da_lambda[layer], da_norm_g[layer].reshape(-1, 1), layer)
        y_c = _spatial_gating(rest, sg_norm_g[layer].reshape(1, -1), sg_norm_b[layer].reshape(1, -1),
                              sg_w_s[layer].astype(BF16), sg_b_s[layer].T)
        merged = _merge(y_a.reshape(m, HG_WIDTH), y_b.reshape(m, DA_V_WIDTH), y_c,
                        wa_b, wb_b, wc_b, layer, gates)
        xf, xb = _mm_ln(merged, w_out_b, layer, xf, ln1_g[layer].reshape(1, -1),
                        ln1_b[layer].reshape(1, -1), name="out_proj_ln")
        hid = _up_proj(xb, w_up, layer)
        xf, xb = _mm_ln(hid, w_down_b, layer, xf, ln2_g[layer].reshape(1, -1),
                        ln2_b[layer].reshape(1, -1), name="down_proj_ln")
    return xf.reshape(bsz, seq, d)
```

```python
import functools
import math

import jax
import jax.numpy as jnp
from jax import lax
from jax.experimental import pallas as pl
from jax.experimental.pallas import tpu as pltpu

D_MODEL = 4096
DEPTH = 2
HG_WIDTH = D_MODEL // 4
HG_DK = 128
HG_HEADS = HG_WIDTH // HG_DK
DA_DH = 128
DA_HEADS = D_MODEL // (4 * DA_DH)
DA_QK_WIDTH = DA_HEADS * 2 * DA_DH
DA_V_WIDTH = DA_HEADS * 2 * DA_DH
ROPE_THETA = 500000.0
ROT_DIM = DA_DH // 4
SG_CHUNK = 128
SG_WIDTH = D_MODEL // 4
SG_GDIM = 128
SG_GROUPS = SG_WIDTH // SG_GDIM
N_BRANCH = 3
D_FF = 4 * D_MODEL
ALPHA = (2.0 * DEPTH) ** 0.25
EPS = 1e-5
D_IN = 5 * HG_WIDTH + 2 * DA_QK_WIDTH + DA_V_WIDTH + 2 * SG_WIDTH

COL_AQ, COL_AFF, COL_AFB, COL_AI, COL_AG = (i * HG_WIDTH for i in range(5))
COL_BQ = 5 * HG_WIDTH
COL_BK = COL_BQ + DA_QK_WIDTH
COL_BV = COL_BK + DA_QK_WIDTH
COL_CU = COL_BV + DA_V_WIDTH
COL_CV = COL_CU + SG_WIDTH

QK_Q, QK_K = 0, DA_QK_WIDTH
REST_WIDTH = D_IN - 2 * DA_QK_WIDTH
REST_BV = COL_BV - 2 * DA_QK_WIDTH
REST_CU = COL_CU - 2 * DA_QK_WIDTH
REST_CV = COL_CV - 2 * DA_QK_WIDTH

LANES = 128
VMEM_LIMIT = 56 * 1024 * 1024
GLA_CHUNK = 64
GLA_SUB = 16
LN_ROWS = 64
ATT_STRIP = 256
GLA_GROUP = 8
LOG2E = math.log2(math.e)
MASKED_EXP = -1e30

F32 = jnp.float32
BF16 = jnp.bfloat16


def _params(sem):
    return pltpu.CompilerParams(dimension_semantics=sem, vmem_limit_bytes=VMEM_LIMIT)


def _dot(a, b):
    return jnp.dot(a, b, preferred_element_type=F32)


def _dot_nt(a, b):
    return lax.dot_general(a, b, (((1,), (1,)), ((), ())), preferred_element_type=F32)


def _dot_tn(a, b):
    return lax.dot_general(a, b, (((0,), (0,)), ((), ())), preferred_element_type=F32)


def _sigmoid(x):
    return 0.5 * jnp.tanh(0.5 * x) + 0.5


def _proj_qk_body(x_ref, w_ref, c_ref, s1_ref, s2_ref, o_ref, acc_a, acc_b, *, steps):
    t = pl.program_id(0)

    def rotary(prev_ref):
        c = c_ref[...]
        s1 = s1_ref[...]
        s2 = s2_ref[...]
        for g in range(o_ref.shape[1] // LANES):
            v = prev_ref[:, g * LANES:(g + 1) * LANES]
            r = (v * c + pltpu.roll(v, ROT_DIM // 2, 1) * s1
                 + pltpu.roll(v, LANES - ROT_DIM // 2, 1) * s2)
            o_ref[:, g * LANES:(g + 1) * LANES] = r.astype(o_ref.dtype)

    @pl.when(t == 0)
    def _():
        acc_b[...] = jnp.zeros_like(acc_b)

    for parity, (cur_ref, prev_ref) in enumerate(((acc_a, acc_b), (acc_b, acc_a))):
        @pl.when(jnp.logical_and(t < steps, t % 2 == parity))
        def _(cur_ref=cur_ref, prev_ref=prev_ref):
            cur_ref[...] = _dot(x_ref[...], w_ref[...].astype(BF16))
            rotary(prev_ref)

    @pl.when(t == steps)
    def _():
        rotary(acc_a if (steps - 1) % 2 == 0 else acc_b)


def _proj_qk(h, w, layer, rot_c, rot_s1, rot_s2, seq, *, tm=1024, tn=512):
    m, k = h.shape
    tm = min(tm, seq)
    nseq = seq // tm
    nq = DA_QK_WIDTH // tn
    ncol = 2 * nq
    first = COL_BQ // tn
    steps = (m // tm) * ncol

    def cur(t):
        tc = jnp.minimum(t, steps - 1)
        return tc // ncol, tc % ncol

    def lag(t):
        tl = jnp.maximum(t - 1, 0)
        return tl // ncol, tl % ncol

    tab = pl.BlockSpec((None, tm, LANES), lambda t: (lag(t)[1] // nq, lag(t)[0] % nseq, 0))
    return pl.pallas_call(
        functools.partial(_proj_qk_body, steps=steps),
        out_shape=jax.ShapeDtypeStruct((m, 2 * DA_QK_WIDTH), BF16),
        grid=(steps + 1,),
        in_specs=[pl.BlockSpec((tm, k), lambda t: (cur(t)[0], 0)),
                  pl.BlockSpec((None, k, tn), lambda t: (layer, 0, first + cur(t)[1])),
                  tab, tab, tab],
        out_specs=pl.BlockSpec((tm, tn), lag),
        scratch_shapes=[pltpu.VMEM((tm, tn), F32), pltpu.VMEM((tm, tn), F32)],
        compiler_params=_params(("arbitrary",)),
        name="proj_qk",
    )(h, w, rot_c, rot_s1, rot_s2)


def _proj_rest_body(x_ref, w_ref, o_ref):
    o_ref[...] = _dot(x_ref[...], w_ref[...].astype(BF16)).astype(o_ref.dtype)


def _proj_rest(h, w, layer, *, tm=1024, tn=512):
    m, k = h.shape
    tm = min(tm, m)
    first_skip = COL_BQ // tn
    skipped = 2 * DA_QK_WIDTH // tn
    return pl.pallas_call(
        _proj_rest_body,
        out_shape=jax.ShapeDtypeStruct((m, REST_WIDTH), BF16),
        grid=(m // tm, REST_WIDTH // tn),
        in_specs=[pl.BlockSpec((tm, k), lambda i, j: (i, 0)),
                  pl.BlockSpec((None, k, tn),
                               lambda i, j: (layer, 0, j + jnp.where(j >= first_skip, skipped, 0)))],
        out_specs=pl.BlockSpec((tm, tn), lambda i, j: (i, j)),
        compiler_params=_params(("parallel", "arbitrary")),
        name="proj_rest",
    )(h, w)


def _gate_body(x_ref, w_ref, b_ref, o_ref):
    acc = _dot(x_ref[...], w_ref[...].astype(BF16)) + b_ref[...]
    o_ref[...] = _sigmoid(acc).astype(o_ref.dtype)


def _gate_proj(h, w, layer, b, *, tm=1024, tn=512):
    m, k = h.shape
    n = w.shape[2]
    tm = min(tm, m)
    return pl.pallas_call(
        _gate_body,
        out_shape=jax.ShapeDtypeStruct((m, n), BF16),
        grid=(m // tm, n // tn),
        in_specs=[pl.BlockSpec((tm, k), lambda i, j: (i, 0)),
                  pl.BlockSpec((None, k, tn), lambda i, j: (layer, 0, j)),
                  pl.BlockSpec((1, tn), lambda i, j: (0, j))],
        out_specs=pl.BlockSpec((tm, tn), lambda i, j: (i, j)),
        compiler_params=_params(("parallel", "arbitrary")),
        name="gate_proj",
    )(h, w, b)


def _up_body(x_ref, w_ref, o_ref):
    r = jnp.maximum(_dot(x_ref[...], w_ref[...].astype(BF16)), 0.0)
    o_ref[...] = (r * r).astype(o_ref.dtype)


def _up_proj(h, w, layer, *, tm=1024, tn=512):
    m, k = h.shape
    n = w.shape[2]
    tm = min(tm, m)
    return pl.pallas_call(
        _up_body,
        out_shape=jax.ShapeDtypeStruct((m, n), BF16),
        grid=(m // tm, n // tn),
        in_specs=[pl.BlockSpec((tm, k), lambda i, j: (i, 0)),
                  pl.BlockSpec((None, k, tn), lambda i, j: (layer, 0, j))],
        out_specs=pl.BlockSpec((tm, tn), lambda i, j: (i, j)),
        compiler_params=_params(("parallel", "arbitrary")),
        name="up_proj",
    )(h, w)


def _mm_ln_body(y_ref, w_ref, res_ref, g_ref, b_ref, of_ref, ob_ref, acc_a, acc_b, *, nk, nt):
    i = pl.program_id(0)
    kk = pl.program_id(1)
    slab = res_ref.shape[0]
    rows = min(slab, LN_ROWS)

    def normalise(prev_ref):
        g = g_ref[...]
        b = b_ref[...]
        for r0 in range(0, slab, rows):
            rs = pl.ds(pl.multiple_of(kk * slab + r0, rows), rows)
            t = prev_ref[rs, :] + ALPHA * res_ref[r0:r0 + rows, :]
            prev_ref[rs, :] = jnp.zeros((rows, prev_ref.shape[1]), F32)
            mu = jnp.mean(t, axis=-1, keepdims=True)
            d = t - mu
            var = jnp.mean(d * d, axis=-1, keepdims=True)
            out = d * lax.rsqrt(var + EPS) * g + b
            of_ref[r0:r0 + rows, :] = out
            ob_ref[r0:r0 + rows, :] = out.astype(ob_ref.dtype)

    @pl.when(jnp.logical_and(i == 0, kk == 0))
    def _():
        acc_a[...] = jnp.zeros_like(acc_a)
        acc_b[...] = jnp.zeros_like(acc_b)

    for parity, (cur_ref, prev_ref) in enumerate(((acc_a, acc_b), (acc_b, acc_a))):
        @pl.when(jnp.logical_and(i < nt, i % 2 == parity))
        def _(cur_ref=cur_ref, prev_ref=prev_ref):
            cur_ref[...] += _dot(y_ref[...], w_ref[...])
            normalise(prev_ref)

    @pl.when(i == nt)
    def _():
        normalise(acc_a if (nt - 1) % 2 == 0 else acc_b)


def _mm_ln(y, w, layer, res, g, b, *, tm=512, tk=1024, name):
    m, k = y.shape
    n = w.shape[2]
    tm = min(tm, m)
    nk = k // tk
    nt = m // tm
    slab = tm // nk

    def lagged(i, kk):
        return (jnp.where(i == 0, 0, (i - 1) * nk + kk), 0)

    def last_k(i, kk):
        return jnp.where(i < nt, kk, nk - 1)

    return pl.pallas_call(
        functools.partial(_mm_ln_body, nk=nk, nt=nt),
        out_shape=(jax.ShapeDtypeStruct((m, n), F32), jax.ShapeDtypeStruct((m, n), BF16)),
        grid=(nt + 1, nk),
        in_specs=[pl.BlockSpec((tm, tk), lambda i, kk: (jnp.minimum(i, nt - 1), last_k(i, kk))),
                  pl.BlockSpec((None, tk, n), lambda i, kk: (layer, last_k(i, kk), 0)),
                  pl.BlockSpec((slab, n), lagged),
                  pl.BlockSpec((1, n), lambda i, kk: (0, 0)),
                  pl.BlockSpec((1, n), lambda i, kk: (0, 0))],
        out_specs=(pl.BlockSpec((slab, n), lagged),
                   pl.BlockSpec((slab, n), lagged)),
        scratch_shapes=[pltpu.VMEM((tm, n), F32), pltpu.VMEM((tm, n), F32)],
        compiler_params=_params(("arbitrary", "arbitrary")),
        name=name,
    )(y, w, res, g, b)


def _merge_body(ya_ref, yb_ref, yc_ref, wa_ref, wb_ref, wc_ref, ga_ref, gb_ref, gc_ref, o_ref):
    acc = ga_ref[...].astype(F32) * _dot(ya_ref[...], wa_ref[...])
    acc += gb_ref[...].astype(F32) * _dot(yb_ref[...], wb_ref[...])
    acc += gc_ref[...].astype(F32) * _dot(yc_ref[...], wc_ref[...])
    o_ref[...] = acc.astype(o_ref.dtype)


def _merge(ya, yb, yc, wa, wb, wc, layer, gates, *, tm=1024, tn=512):
    m = ya.shape[0]
    n = wa.shape[2]
    tm = min(tm, m)
    nb = n // tn

    def act(width):
        return pl.BlockSpec((tm, width), lambda i, j: (i, 0))

    def wt(width):
        return pl.BlockSpec((None, width, tn), lambda i, j: (layer, 0, j))

    def gate(branch):
        return pl.BlockSpec((tm, tn), lambda i, j: (i, branch * nb + j))

    return pl.pallas_call(
        _merge_body,
        out_shape=jax.ShapeDtypeStruct((m, n), BF16),
        grid=(m // tm, nb),
        in_specs=[act(ya.shape[1]), act(yb.shape[1]), act(yc.shape[1]),
                  wt(wa.shape[1]), wt(wb.shape[1]), wt(wc.shape[1]),
                  gate(0), gate(1), gate(2)],
        out_specs=pl.BlockSpec((tm, tn), lambda i, j: (i, j)),
        compiler_params=_params(("parallel", "arbitrary")),
        name="merge",
    )(ya, yb, yc, wa, wb, wc, gates, gates, gates)


def _attn_body(q_ref, k_ref, v_ref, lam_ref, g_ref, o_ref, qt_ref, vt_ref, m_ref, l_ref, acc_ref,
               *, tk, lam_init):
    seq = k_ref.shape[0]
    tq = q_ref.shape[0]
    nkv = seq // tk

    @pl.when(pl.program_id(2) == 0)
    def _():
        def transpose_values(i, carry):
            r = pl.ds(pl.multiple_of(i * tk, tk), tk)
            vt_ref[i] = v_ref[r, :].astype(F32).T.astype(BF16)
            return carry

        lax.fori_loop(0, nkv, transpose_values, 0)

    qt_ref[...] = q_ref[...].astype(F32).T.astype(BF16)
    m_ref[...] = jnp.full_like(m_ref, -jnp.inf)
    l_ref[...] = jnp.zeros_like(l_ref)
    acc_ref[...] = jnp.zeros_like(acc_ref)

    def step(i, carry):
        rows = pl.ds(pl.multiple_of(i * tk, tk), tk)
        vt = vt_ref[i]
        for j in range(tq // ATT_STRIP):
            for c in range(2):
                kc = k_ref[rows, c * DA_DH:(c + 1) * DA_DH]
                cols = slice(j * ATT_STRIP, (j + 1) * ATT_STRIP)
                st = _dot(kc, qt_ref[c * DA_DH:(c + 1) * DA_DH, cols])
                m_old = m_ref[c, :, cols]
                m_new = jnp.maximum(m_old, jnp.max(st, axis=0, keepdims=True))
                alpha = jnp.exp2(m_old - m_new)
                p = jnp.exp2(st - m_new)
                l_ref[c, :, cols] = alpha * l_ref[c, :, cols] + jnp.sum(p, axis=0, keepdims=True)
                acc_ref[c, :, cols] = alpha * acc_ref[c, :, cols] + _dot(vt, p.astype(BF16))
                m_ref[c, :, cols] = m_new
        return carry

    lax.fori_loop(0, nkv, step, 0, unroll=16)

    lp = lam_ref[...]
    lam = (jnp.exp(jnp.sum(lp[0:1] * lp[1:2], axis=-1, keepdims=True))
           - jnp.exp(jnp.sum(lp[2:3] * lp[3:4], axis=-1, keepdims=True)) + lam_init)
    o = acc_ref[0] * (1.0 / l_ref[0]) - acc_ref[1] * (lam / l_ref[1])
    ms = jnp.mean(o * o, axis=0, keepdims=True)
    o = o * (lax.rsqrt(ms + EPS) * (1.0 - lam_init)) * g_ref[...]
    o_ref[...] = o.T.astype(o_ref.dtype)


def _attention(qk3, rest3, lam_params, norm_g_col, layer, *, tq=1024, tk=256):
    bsz, seq, _ = qk3.shape
    tq = min(tq, seq)
    tk = min(tk, seq)
    hw = 2 * DA_DH
    lam_init = 0.8 - 0.6 * math.exp(-0.3 * layer)
    return pl.pallas_call(
        functools.partial(_attn_body, tk=tk, lam_init=lam_init),
        out_shape=jax.ShapeDtypeStruct((bsz, seq, DA_V_WIDTH), BF16),
        grid=(bsz, DA_HEADS, seq // tq),
        in_specs=[pl.BlockSpec((None, tq, hw), lambda b, h, i: (b, i, QK_Q // hw + h)),
                  pl.BlockSpec((None, seq, hw), lambda b, h, i: (b, 0, QK_K // hw + h)),
                  pl.BlockSpec((None, seq, hw), lambda b, h, i: (b, 0, REST_BV // hw + h)),
                  pl.BlockSpec((4, DA_DH), lambda b, h, i: (0, 0)),
                  pl.BlockSpec((hw, 1), lambda b, h, i: (0, 0))],
        out_specs=pl.BlockSpec((None, tq, hw), lambda b, h, i: (b, i, h)),
        scratch_shapes=[pltpu.VMEM((hw, tq), BF16),
                        pltpu.VMEM((seq // tk, hw, tk), BF16),
                        pltpu.VMEM((2, 1, tq), F32),
                        pltpu.VMEM((2, 1, tq), F32),
                        pltpu.VMEM((2, hw, tq), F32)],
        compiler_params=_params(("parallel", "parallel", "arbitrary")),
        name="diff_attention",
    )(qk3, qk3, rest3, lam_params, norm_g_col)


def _log2(x):
    return jnp.log(x) * LOG2E


def _gla_gates(x, lb):
    xs = x * LOG2E
    e = jnp.exp2(-jnp.abs(xs))
    ls = jnp.minimum(xs, 0.0) - _log2(1.0 + e)
    ls_neg = ls - xs
    if lb is None:
        return ls_neg, ls
    a = _log2(lb)
    rest = _log2(1.0 - lb)
    b = rest + ls
    mx = jnp.maximum(a, b)
    log_f = mx + _log2(jnp.exp2(a - mx) + jnp.exp2(b - mx))
    return rest + ls_neg, log_f


def _gla_chunk(q, x, v, lb, consts, reverse):
    tri, sel, blk_eq, row, keeps = consts
    cs, sub = GLA_CHUNK, GLA_SUB
    nsub = cs // sub
    log_k, g = _gla_gates(x, lb)
    yield None
    b = jnp.dot(tri, g, precision=lax.Precision.HIGHEST, preferred_element_type=F32)
    yield None
    edge = b[0:1] if reverse else b[cs - 1:cs]
    vb = v.astype(BF16)
    beta = b - log_k

    q_dec = (q * jnp.exp2(b)).astype(BF16)
    k_edge = jnp.exp2(edge - beta).astype(BF16)
    decay = jnp.exp2(edge)

    qs, ks = [], []
    for j in range(nsub):
        if (reverse and j == 0) or (not reverse and j == nsub - 1):
            continue
        lo, hi = j * sub, (j + 1) * sub
        anchor = b[lo:lo + 1] if reverse else b[hi - 1:hi]
        q_ok = (row < lo) if reverse else (row >= hi)
        k_ok = jnp.logical_and(row >= lo, row < hi)
        qs.append((q * jnp.exp2(jnp.where(q_ok, b - anchor, MASKED_EXP))).astype(BF16))
        ks.append(jnp.exp2(jnp.where(k_ok, anchor - beta, MASKED_EXP)).astype(BF16))
    yield None

    cols = []
    for s in range(sub):
        keep = keeps[s]
        parts = []
        for j in range(nsub):
            lo = j * sub
            bb = b[lo:lo + sub]
            w = jnp.exp2(jnp.where(keep, bb - beta[lo + s:lo + s + 1], MASKED_EXP))
            parts.append((q[lo:lo + sub] * w).astype(BF16))
        cols.append(jnp.concatenate(parts, axis=0))
    yield None

    a_mat = _dot_nt(jnp.concatenate(qs, axis=1), jnp.concatenate(ks, axis=1))
    diag = _dot(jnp.concatenate(cols, axis=1), sel)
    delta = _dot_tn(vb, k_edge)
    yield None

    a_mat = a_mat + jnp.where(blk_eq, diag, 0.0)
    o_intra = _dot(a_mat.astype(BF16), vb)
    yield o_intra, q_dec, decay, delta


def _gla_body(q_ref, ff_ref, fb_ref, i_ref, g_ref, lbraw_ref, ng_ref, o_ref,
              of_ref, ob_ref, st_ref, *, layer):
    seq = q_ref.shape[0]
    cs, sub = GLA_CHUNK, GLA_SUB
    nc = seq // cs

    if layer == 0:
        lbs = (None, None)
    else:
        raw = lbraw_ref[...]
        e = jnp.exp(raw - jnp.max(raw, axis=0, keepdims=True))
        p = e / jnp.sum(e, axis=0, keepdims=True)
        lb2 = jnp.sum(p[1:layer + 1], axis=0)
        lbs = (lb2[0:1], lb2[1:2])

    r_i = lax.broadcasted_iota(jnp.int32, (cs, cs), 0)
    c_i = lax.broadcasted_iota(jnp.int32, (cs, cs), 1)
    tri_f = (c_i <= r_i).astype(F32)
    tri_b = (c_i >= r_i).astype(F32)
    blk_eq = (r_i // sub) == (c_i // sub)
    sr = lax.broadcasted_iota(jnp.int32, (sub * HG_DK, cs), 0)
    sc = lax.broadcasted_iota(jnp.int32, (sub * HG_DK, cs), 1)
    sel = ((sr // HG_DK) == (sc % sub)).astype(BF16)
    row = lax.broadcasted_iota(jnp.int32, (cs, 1), 0)

    st_ref[...] = jnp.zeros_like(st_ref)

    group = min(GLA_GROUP, nc)
    sub_row = lax.broadcasted_iota(jnp.int32, (sub, HG_DK), 0)
    keeps_f = [sub_row >= s for s in range(sub)]
    keeps_b = [sub_row <= s for s in range(sub)]
    directions = ((ff_ref, lbs[0], (tri_f, sel, blk_eq, row, keeps_f), False, of_ref),
                  (fb_ref, lbs[1], (tri_b, sel, blk_eq, row, keeps_b), True, ob_ref))

    def step(i, carry):
        chains, slots = [], []
        for u in range(group):
            c = i * group + u
            for d, (f_ref, lb, consts, reverse, out_ref) in enumerate(directions):
                r = pl.ds(pl.multiple_of((nc - 1 - c if reverse else c) * cs, cs), cs)
                chains.append(_gla_chunk(q_ref[r, :].astype(F32), f_ref[r, :].astype(F32),
                                         i_ref[r, :].astype(F32), lb, consts, reverse))
                slots.append((d, r, out_ref))
        for results in zip(*chains):
            pass
        states = [st_ref[0], st_ref[1]]
        for (d, r, out_ref), (o_intra, q_dec, decay, delta) in zip(slots, results):
            out_ref[r, :] = o_intra + _dot_nt(q_dec, states[d].astype(BF16))
            states[d] = decay * states[d] + delta
        st_ref[0] = states[0]
        st_ref[1] = states[1]
        return carry

    lax.fori_loop(0, nc // group, step, 0)

    tr = min(512, seq)
    ng = ng_ref[...]

    def fin(i, carry):
        r = pl.ds(pl.multiple_of(i * tr, tr), tr)
        o = of_ref[r, :] + ob_ref[r, :]
        ms = jnp.mean(o * o, axis=-1, keepdims=True)
        y = o * lax.rsqrt(ms + EPS) * ng
        gate = g_ref[r, :].astype(F32)
        o_ref[r, :] = (y * gate * _sigmoid(gate)).astype(o_ref.dtype)
        return carry

    lax.fori_loop(0, seq // tr, fin, 0)


def _hgrn2(proj3, lb_raw, norm_g, layer):
    bsz, seq, _ = proj3.shape

    def col(base):
        return pl.BlockSpec((None, seq, HG_DK), lambda b, h: (b, 0, base // HG_DK + h))

    return pl.pallas_call(
        functools.partial(_gla_body, layer=layer),
        out_shape=jax.ShapeDtypeStruct((bsz, seq, HG_WIDTH), BF16),
        grid=(bsz, HG_HEADS),
        in_specs=[col(COL_AQ), col(COL_AFF), col(COL_AFB), col(COL_AI), col(COL_AG),
                  pl.BlockSpec((DEPTH, 2, HG_DK), lambda b, h: (0, 0, h)),
                  pl.BlockSpec((1, HG_DK), lambda b, h: (0, h))],
        out_specs=pl.BlockSpec((None, seq, HG_DK), lambda b, h: (b, 0, h)),
        scratch_shapes=[pltpu.VMEM((seq, HG_DK), F32),
                        pltpu.VMEM((seq, HG_DK), F32),
                        pltpu.VMEM((2, HG_DK, HG_DK), F32)],
        compiler_params=_params(("parallel", "parallel")),
        name="hgrn2",
    )(proj3, proj3, proj3, proj3, proj3, lb_raw, norm_g)


def _gelu(x):
    return 0.5 * x * (1.0 + jnp.tanh(math.sqrt(2.0 / math.pi) * (x + 0.044715 * (x * x * x))))


def _sgu_body(u_ref, v_ref, ng_ref, nb_ref, ws_ref, bs_ref, o_ref):
    tm = u_ref.shape[0]
    v = _gelu(v_ref[...].astype(F32))
    mu = jnp.mean(v, axis=-1, keepdims=True)
    d = v - mu
    var = jnp.mean(d * d, axis=-1, keepdims=True)
    vn = (d * lax.rsqrt(var + EPS) * ng_ref[...] + nb_ref[...]).astype(BF16)
    bs = bs_ref[...]
    for c in range(tm // SG_CHUNK):
        r0 = c * SG_CHUNK
        for g in range(SG_GROUPS):
            c0 = g * SG_GDIM
            mixed = _dot(ws_ref[g], vn[r0:r0 + SG_CHUNK, c0:c0 + SG_GDIM]) + bs[:, g:g + 1]
            u = _gelu(u_ref[r0:r0 + SG_CHUNK, c0:c0 + SG_GDIM].astype(F32))
            o_ref[r0:r0 + SG_CHUNK, c0:c0 + SG_GDIM] = (u * mixed).astype(o_ref.dtype)


def _spatial_gating(proj, norm_g, norm_b, w_s, b_s_t, *, tm=256):
    m = proj.shape[0]
    return pl.pallas_call(
        _sgu_body,
        out_shape=jax.ShapeDtypeStruct((m, SG_WIDTH), BF16),
        grid=(m // tm,),
        in_specs=[pl.BlockSpec((tm, SG_WIDTH), lambda i: (i, REST_CU // SG_WIDTH)),
                  pl.BlockSpec((tm, SG_WIDTH), lambda i: (i, REST_CV // SG_WIDTH)),
                  pl.BlockSpec((1, SG_WIDTH), lambda i: (0, 0)),
                  pl.BlockSpec((1, SG_WIDTH), lambda i: (0, 0)),
                  pl.BlockSpec((SG_GROUPS, SG_CHUNK, SG_CHUNK), lambda i: (0, 0, 0)),
                  pl.BlockSpec((SG_CHUNK, SG_GROUPS), lambda i: (0, 0))],
        out_specs=pl.BlockSpec((tm, SG_WIDTH), lambda i: (i, 0)),
        compiler_params=_params(("parallel",)),
        name="spatial_gating",
    )(proj, proj, norm_g, norm_b, w_s, b_s_t)


def _rotary_tables(seq):
    half = ROT_DIM // 2
    pos = jnp.arange(seq, dtype=F32)
    freqs = ROPE_THETA ** (-jnp.arange(0, ROT_DIM, 2, dtype=F32) / ROT_DIM)
    ang = pos[:, None] * freqs[None, :]
    cos, sin = jnp.cos(ang), jnp.sin(ang)
    rest = LANES - ROT_DIM
    c = jnp.concatenate([cos, cos, jnp.ones((seq, rest), F32)], axis=1)
    s1 = jnp.concatenate([jnp.zeros((seq, half), F32), sin, jnp.zeros((seq, rest), F32)], axis=1)
    s2 = jnp.concatenate([-sin, jnp.zeros((seq, LANES - half), F32)], axis=1)
    q_scale = LOG2E * DA_DH ** -0.5
    return tuple(jnp.stack([t * q_scale, t]) for t in (c, s1, s2))


def kernel(x, w_in, hg_lb_raw, hg_norm_g, da_lambda, da_norm_g, sg_norm_g, sg_norm_b, sg_w_s, sg_b_s, w_branch_a, w_branch_b, w_branch_c, w_gate, b_gate, w_out, ln1_g, ln1_b, w_up, w_down, ln2_g, ln2_b):
    bsz, seq, d = x.shape
    m = bsz * seq
    rot_c, rot_s1, rot_s2 = _rotary_tables(seq)
    xf = x.reshape(m, d)
    xb = xf.astype(BF16)
    wa_b, wb_b, wc_b = (w.astype(BF16) for w in (w_branch_a, w_branch_b, w_branch_c))
    w_out_b = w_out.astype(BF16)
    w_down_b = w_down.astype(BF16)
    for layer in range(DEPTH):
        qk = _proj_qk(xb, w_in, layer, rot_c, rot_s1, rot_s2, seq)
        rest = _proj_rest(xb, w_in, layer)
        gates = _gate_proj(xb, w_gate, layer, b_gate[layer].reshape(1, -1))
        qk3 = qk.reshape(bsz, seq, 2 * DA_QK_WIDTH)
        rest3 = rest.reshape(bsz, seq, REST_WIDTH)
        y_a = _hgrn2(rest3, hg_lb_raw, hg_norm_g[layer].reshape(1, -1), layer)
        y_b = _attention(qk3, rest3, da_lambda[layer], da_norm_g[layer].reshape(-1, 1), layer)
        y_c = _spatial_gating(rest, sg_norm_g[layer].reshape(1, -1), sg_norm_b[layer].reshape(1, -1),
                              sg_w_s[layer].astype(BF16), sg_b_s[layer].T)
        merged = _merge(y_a.reshape(m, HG_WIDTH), y_b.reshape(m, DA_V_WIDTH), y_c,
                        wa_b, wb_b, wc_b, layer, gates)
        xf, xb = _mm_ln(merged, w_out_b, layer, xf, ln1_g[layer].reshape(1, -1),
                        ln1_b[layer].reshape(1, -1), name="out_proj_ln")
        hid = _up_proj(xb, w_up, layer)
        xf, xb = _mm_ln(hid, w_down_b, layer, xf, ln2_g[layer].reshape(1, -1),
                        ln2_b[layer].reshape(1, -1), name="down_proj_ln")
    return xf.reshape(bsz, seq, d)
```

```python
import functools
import math

import jax
import jax.numpy as jnp
from jax import lax
from jax.experimental import pallas as pl
from jax.experimental.pallas import tpu as pltpu

D_MODEL = 4096
DEPTH = 2
HG_WIDTH = D_MODEL // 4
HG_DK = 128
HG_HEADS = HG_WIDTH // HG_DK
DA_DH = 128
DA_HEADS = D_MODEL // (4 * DA_DH)
DA_QK_WIDTH = DA_HEADS * 2 * DA_DH
DA_V_WIDTH = DA_HEADS * 2 * DA_DH
ROPE_THETA = 500000.0
ROT_DIM = DA_DH // 4
SG_CHUNK = 128
SG_WIDTH = D_MODEL // 4
SG_GDIM = 128
SG_GROUPS = SG_WIDTH // SG_GDIM
N_BRANCH = 3
D_FF = 4 * D_MODEL
ALPHA = (2.0 * DEPTH) ** 0.25
EPS = 1e-5
D_IN = 5 * HG_WIDTH + 2 * DA_QK_WIDTH + DA_V_WIDTH + 2 * SG_WIDTH

COL_AQ, COL_AFF, COL_AFB, COL_AI, COL_AG = (i * HG_WIDTH for i in range(5))
COL_BQ = 5 * HG_WIDTH
COL_BK = COL_BQ + DA_QK_WIDTH
COL_BV = COL_BK + DA_QK_WIDTH
COL_CU = COL_BV + DA_V_WIDTH
COL_CV = COL_CU + SG_WIDTH

QK_Q, QK_K = 0, DA_QK_WIDTH
REST_WIDTH = D_IN - 2 * DA_QK_WIDTH
REST_BV = COL_BV - 2 * DA_QK_WIDTH
REST_CU = COL_CU - 2 * DA_QK_WIDTH
REST_CV = COL_CV - 2 * DA_QK_WIDTH

LANES = 128
VMEM_LIMIT = 56 * 1024 * 1024
GLA_CHUNK = 64
GLA_SUB = 16
LN_ROWS = 64
ATT_STRIP = 256
GLA_GROUP = 8
LOG2E = math.log2(math.e)
MASKED_EXP = -1e30

F32 = jnp.float32
BF16 = jnp.bfloat16


def _params(sem):
    return pltpu.CompilerParams(dimension_semantics=sem, vmem_limit_bytes=VMEM_LIMIT)


def _dot(a, b):
    return jnp.dot(a, b, preferred_element_type=F32)


def _dot_nt(a, b):
    return lax.dot_general(a, b, (((1,), (1,)), ((), ())), preferred_element_type=F32)


def _dot_tn(a, b):
    return lax.dot_general(a, b, (((0,), (0,)), ((), ())), preferred_element_type=F32)


def _sigmoid(x):
    return 0.5 * jnp.tanh(0.5 * x) + 0.5


def _proj_qk_body(x_ref, w_ref, c_ref, s1_ref, s2_ref, o_ref, acc_a, acc_b, *, steps):
    t = pl.program_id(0)

    def rotary(prev_ref):
        c = c_ref[...]
        s1 = s1_ref[...]
        s2 = s2_ref[...]
        for g in range(o_ref.shape[1] // LANES):
            v = prev_ref[:, g * LANES:(g + 1) * LANES]
            r = (v * c + pltpu.roll(v, ROT_DIM // 2, 1) * s1
                 + pltpu.roll(v, LANES - ROT_DIM // 2, 1) * s2)
            o_ref[:, g * LANES:(g + 1) * LANES] = r.astype(o_ref.dtype)

    @pl.when(t == 0)
    def _():
        acc_b[...] = jnp.zeros_like(acc_b)

    for parity, (cur_ref, prev_ref) in enumerate(((acc_a, acc_b), (acc_b, acc_a))):
        @pl.when(jnp.logical_and(t < steps, t % 2 == parity))
        def _(cur_ref=cur_ref, prev_ref=prev_ref):
            cur_ref[...] = _dot(x_ref[...], w_ref[...].astype(BF16))
            rotary(prev_ref)

    @pl.when(t == steps)
    def _():
        rotary(acc_a if (steps - 1) % 2 == 0 else acc_b)


def _proj_qk(h, w, layer, rot_c, rot_s1, rot_s2, seq, *, tm=1024, tn=512):
    m, k = h.shape
    tm = min(tm, seq)
    nseq = seq // tm
    nq = DA_QK_WIDTH // tn
    ncol = 2 * nq
    first = COL_BQ // tn
    steps = (m // tm) * ncol

    def cur(t):
        tc = jnp.minimum(t, steps - 1)
        return tc // ncol, tc % ncol

    def lag(t):
        tl = jnp.maximum(t - 1, 0)
        return tl // ncol, tl % ncol

    tab = pl.BlockSpec((None, tm, LANES), lambda t: (lag(t)[1] // nq, lag(t)[0] % nseq, 0))
    return pl.pallas_call(
        functools.partial(_proj_qk_body, steps=steps),
        out_shape=jax.ShapeDtypeStruct((m, 2 * DA_QK_WIDTH), BF16),
        grid=(steps + 1,),
        in_specs=[pl.BlockSpec((tm, k), lambda t: (cur(t)[0], 0)),
                  pl.BlockSpec((None, k, tn), lambda t: (layer, 0, first + cur(t)[1])),
                  tab, tab, tab],
        out_specs=pl.BlockSpec((tm, tn), lag),
        scratch_shapes=[pltpu.VMEM((tm, tn), F32), pltpu.VMEM((tm, tn), F32)],
        compiler_params=_params(("arbitrary",)),
        name="proj_qk",
    )(h, w, rot_c, rot_s1, rot_s2)


def _proj_rest_body(x_ref, w_ref, o_ref):
    o_ref[...] = _dot(x_ref[...], w_ref[...].astype(BF16)).astype(o_ref.dtype)


def _proj_rest(h, w, layer, *, tm=1024, tn=512):
    m, k = h.shape
    tm = min(tm, m)
    first_skip = COL_BQ // tn
    skipped = 2 * DA_QK_WIDTH // tn
    return pl.pallas_call(
        _proj_rest_body,
        out_shape=jax.ShapeDtypeStruct((m, REST_WIDTH), BF16),
        grid=(m // tm, REST_WIDTH // tn),
        in_specs=[pl.BlockSpec((tm, k), lambda i, j: (i, 0)),
                  pl.BlockSpec((None, k, tn),
                               lambda i, j: (layer, 0, j + jnp.where(j >= first_skip, skipped, 0)))],
        out_specs=pl.BlockSpec((tm, tn), lambda i, j: (i, j)),
        compiler_params=_params(("parallel", "arbitrary")),
        name="proj_rest",
    )(h, w)


def _gate_body(x_ref, w_ref, b_ref, o_ref):
    acc = _dot(x_ref[...], w_ref[...].astype(BF16)) + b_ref[...]
    o_ref[...] = _sigmoid(acc).astype(o_ref.dtype)


def _gate_proj(h, w, layer, b, *, tm=1024, tn=512):
    m, k = h.shape
    n = w.shape[2]
    tm = min(tm, m)
    return pl.pallas_call(
        _gate_body,
        out_shape=jax.ShapeDtypeStruct((m, n), BF16),
        grid=(m // tm, n // tn),
        in_specs=[pl.BlockSpec((tm, k), lambda i, j: (i, 0)),
                  pl.BlockSpec((None, k, tn), lambda i, j: (layer, 0, j)),
                  pl.BlockSpec((1, tn), lambda i, j: (0, j))],
        out_specs=pl.BlockSpec((tm, tn), lambda i, j: (i, j)),
        compiler_params=_params(("parallel", "arbitrary")),
        name="gate_proj",
    )(h, w, b)


def _up_body(x_ref, w_ref, r_ref, o_ref, rb_ref):
    r = jnp.maximum(_dot(x_ref[...], w_ref[...].astype(BF16)), 0.0)
    o_ref[...] = (r * r).astype(o_ref.dtype)
    rb_ref[...] = r_ref[...].astype(rb_ref.dtype)


def _up_proj(h, w, layer, rider, *, tm=1024, tn=512):
    m, k = h.shape
    n = w.shape[2]
    tm = min(tm, m)
    nb = n // tn
    r_in, r_out, r_shape = _rider_specs(rider, layer, (m // tm) * nb, lambda i, j: i * nb + j)
    return pl.pallas_call(
        _up_body,
        out_shape=(jax.ShapeDtypeStruct((m, n), BF16), r_shape),
        grid=(m // tm, nb),
        in_specs=[pl.BlockSpec((tm, k), lambda i, j: (i, 0)),
                  pl.BlockSpec((None, k, tn), lambda i, j: (layer, 0, j)),
                  r_in],
        out_specs=(pl.BlockSpec((tm, tn), lambda i, j: (i, j)), r_out),
        compiler_params=_params(("arbitrary", "arbitrary")),
        name="up_proj",
    )(h, w, rider)


def _mm_ln_body(y_ref, w_ref, res_ref, g_ref, b_ref, of_ref, ob_ref, acc_a, acc_b, *, nk, nt):
    i = pl.program_id(0)
    kk = pl.program_id(1)
    slab = res_ref.shape[0]
    rows = min(slab, LN_ROWS)

    def normalise(prev_ref):
        g = g_ref[...]
        b = b_ref[...]
        for r0 in range(0, slab, rows):
            rs = pl.ds(pl.multiple_of(kk * slab + r0, rows), rows)
            t = prev_ref[rs, :] + ALPHA * res_ref[r0:r0 + rows, :]
            prev_ref[rs, :] = jnp.zeros((rows, prev_ref.shape[1]), F32)
            mu = jnp.mean(t, axis=-1, keepdims=True)
            d = t - mu
            var = jnp.mean(d * d, axis=-1, keepdims=True)
            out = d * lax.rsqrt(var + EPS) * g + b
            of_ref[r0:r0 + rows, :] = out
            ob_ref[r0:r0 + rows, :] = out.astype(ob_ref.dtype)

    @pl.when(jnp.logical_and(i == 0, kk == 0))
    def _():
        acc_a[...] = jnp.zeros_like(acc_a)
        acc_b[...] = jnp.zeros_like(acc_b)

    for parity, (cur_ref, prev_ref) in enumerate(((acc_a, acc_b), (acc_b, acc_a))):
        @pl.when(jnp.logical_and(i < nt, i % 2 == parity))
        def _(cur_ref=cur_ref, prev_ref=prev_ref):
            cur_ref[...] += _dot(y_ref[...], w_ref[...])
            normalise(prev_ref)

    @pl.when(i == nt)
    def _():
        normalise(acc_a if (nt - 1) % 2 == 0 else acc_b)


def _mm_ln(y, w, layer, res, g, b, *, tm=512, tk=1024, name):
    m, k = y.shape
    n = w.shape[2]
    tm = min(tm, m)
    nk = k // tk
    nt = m // tm
    slab = tm // nk

    def lagged(i, kk):
        return (jnp.where(i == 0, 0, (i - 1) * nk + kk), 0)

    def last_k(i, kk):
        return jnp.where(i < nt, kk, nk - 1)

    return pl.pallas_call(
        functools.partial(_mm_ln_body, nk=nk, nt=nt),
        out_shape=(jax.ShapeDtypeStruct((m, n), F32), jax.ShapeDtypeStruct((m, n), BF16)),
        grid=(nt + 1, nk),
        in_specs=[pl.BlockSpec((tm, tk), lambda i, kk: (jnp.minimum(i, nt - 1), last_k(i, kk))),
                  pl.BlockSpec((None, tk, n), lambda i, kk: (layer, last_k(i, kk), 0)),
                  pl.BlockSpec((slab, n), lagged),
                  pl.BlockSpec((1, n), lambda i, kk: (0, 0)),
                  pl.BlockSpec((1, n), lambda i, kk: (0, 0))],
        out_specs=(pl.BlockSpec((slab, n), lagged),
                   pl.BlockSpec((slab, n), lagged)),
        scratch_shapes=[pltpu.VMEM((tm, n), F32), pltpu.VMEM((tm, n), F32)],
        compiler_params=_params(("arbitrary", "arbitrary")),
        name=name,
    )(y, w, res, g, b)


def _rider_specs(rider, layer, steps, step_of):
    _, r, c = rider.shape
    slab = r // steps
    assert slab * steps == r and slab % 16 == 0
    return (pl.BlockSpec((None, slab, c), lambda *g: (layer, step_of(*g), 0)),
            pl.BlockSpec((None, slab, c), lambda *g: (0, step_of(*g), 0)),
            jax.ShapeDtypeStruct((1, r, c), BF16))


def _merge_body(ya_ref, yb_ref, yc_ref, wa_ref, wb_ref, wc_ref, ga_ref, gb_ref, gc_ref, r_ref,
                o_ref, rb_ref):
    acc = ga_ref[...].astype(F32) * _dot(ya_ref[...], wa_ref[...].astype(BF16))
    acc += gb_ref[...].astype(F32) * _dot(yb_ref[...], wb_ref[...].astype(BF16))
    acc += gc_ref[...].astype(F32) * _dot(yc_ref[...], wc_ref[...].astype(BF16))
    o_ref[...] = acc.astype(o_ref.dtype)
    rb_ref[...] = r_ref[...].astype(rb_ref.dtype)


def _merge(ya, yb, yc, wa, wb, wc, layer, gates, rider, *, tm=1024, tn=512):
    m = ya.shape[0]
    n = wa.shape[2]
    tm = min(tm, m)
    nb = n // tn
    r_in, r_out, r_shape = _rider_specs(rider, layer, (m // tm) * nb, lambda i, j: i * nb + j)

    def act(width):
        return pl.BlockSpec((tm, width), lambda i, j: (i, 0))

    def wt(width):
        return pl.BlockSpec((None, width, tn), lambda i, j: (layer, 0, j))

    def gate(branch):
        return pl.BlockSpec((tm, tn), lambda i, j: (i, branch * nb + j))

    return pl.pallas_call(
        _merge_body,
        out_shape=(jax.ShapeDtypeStruct((m, n), BF16), r_shape),
        grid=(m // tm, nb),
        in_specs=[act(ya.shape[1]), act(yb.shape[1]), act(yc.shape[1]),
                  wt(wa.shape[1]), wt(wb.shape[1]), wt(wc.shape[1]),
                  gate(0), gate(1), gate(2), r_in],
        out_specs=(pl.BlockSpec((tm, tn), lambda i, j: (i, j)), r_out),
        compiler_params=_params(("arbitrary", "arbitrary")),
        name="merge",
    )(ya, yb, yc, wa, wb, wc, gates, gates, gates, rider)


def _attn_body(q_ref, k_ref, v_ref, lam_ref, g_ref, o_ref, qt_ref, vt_ref, m_ref, l_ref, acc_ref,
               *, tk, lam_init):
    seq = k_ref.shape[0]
    tq = q_ref.shape[0]
    nkv = seq // tk

    @pl.when(pl.program_id(2) == 0)
    def _():
        def transpose_values(i, carry):
            r = pl.ds(pl.multiple_of(i * tk, tk), tk)
            vt_ref[i] = v_ref[r, :].astype(F32).T.astype(BF16)
            return carry

        lax.fori_loop(0, nkv, transpose_values, 0)

    qt_ref[...] = q_ref[...].astype(F32).T.astype(BF16)
    m_ref[...] = jnp.full_like(m_ref, -jnp.inf)
    l_ref[...] = jnp.zeros_like(l_ref)
    acc_ref[...] = jnp.zeros_like(acc_ref)

    def step(i, carry):
        rows = pl.ds(pl.multiple_of(i * tk, tk), tk)
        vt = vt_ref[i]
        for j in range(tq // ATT_STRIP):
            for c in range(2):
                kc = k_ref[rows, c * DA_DH:(c + 1) * DA_DH]
                cols = slice(j * ATT_STRIP, (j + 1) * ATT_STRIP)
                st = _dot(kc, qt_ref[c * DA_DH:(c + 1) * DA_DH, cols])
                m_old = m_ref[c, :, cols]
                m_new = jnp.maximum(m_old, jnp.max(st, axis=0, keepdims=True))
                alpha = jnp.exp2(m_old - m_new)
                p = jnp.exp2(st - m_new)
                l_ref[c, :, cols] = alpha * l_ref[c, :, cols] + jnp.sum(p, axis=0, keepdims=True)
                acc_ref[c, :, cols] = alpha * acc_ref[c, :, cols] + _dot(vt, p.astype(BF16))
                m_ref[c, :, cols] = m_new
        return carry

    lax.fori_loop(0, nkv, step, 0, unroll=16)

    lp = lam_ref[...]
    lam = (jnp.exp(jnp.sum(lp[0:1] * lp[1:2], axis=-1, keepdims=True))
           - jnp.exp(jnp.sum(lp[2:3] * lp[3:4], axis=-1, keepdims=True)) + lam_init)
    o = acc_ref[0] * (1.0 / l_ref[0]) - acc_ref[1] * (lam / l_ref[1])
    ms = jnp.mean(o * o, axis=0, keepdims=True)
    o = o * (lax.rsqrt(ms + EPS) * (1.0 - lam_init)) * g_ref[...]
    o_ref[...] = o.T.astype(o_ref.dtype)


def _attention(qk3, rest3, lam_params, norm_g_col, layer, *, tq=1024, tk=256):
    bsz, seq, _ = qk3.shape
    tq = min(tq, seq)
    tk = min(tk, seq)
    hw = 2 * DA_DH
    lam_init = 0.8 - 0.6 * math.exp(-0.3 * layer)
    return pl.pallas_call(
        functools.partial(_attn_body, tk=tk, lam_init=lam_init),
        out_shape=jax.ShapeDtypeStruct((bsz, seq, DA_V_WIDTH), BF16),
        grid=(bsz, DA_HEADS, seq // tq),
        in_specs=[pl.BlockSpec((None, tq, hw), lambda b, h, i: (b, i, QK_Q // hw + h)),
                  pl.BlockSpec((None, seq, hw), lambda b, h, i: (b, 0, QK_K // hw + h)),
                  pl.BlockSpec((None, seq, hw), lambda b, h, i: (b, 0, REST_BV // hw + h)),
                  pl.BlockSpec((4, DA_DH), lambda b, h, i: (0, 0)),
                  pl.BlockSpec((hw, 1), lambda b, h, i: (0, 0))],
        out_specs=pl.BlockSpec((None, tq, hw), lambda b, h, i: (b, i, h)),
        scratch_shapes=[pltpu.VMEM((hw, tq), BF16),
                        pltpu.VMEM((seq // tk, hw, tk), BF16),
                        pltpu.VMEM((2, 1, tq), F32),
                        pltpu.VMEM((2, 1, tq), F32),
                        pltpu.VMEM((2, hw, tq), F32)],
        compiler_params=_params(("parallel", "parallel", "arbitrary")),
        name="diff_attention",
    )(qk3, qk3, rest3, lam_params, norm_g_col)


def _log2(x):
    return jnp.log(x) * LOG2E


def _gla_gates(x, lb):
    xs = x * LOG2E
    e = jnp.exp2(-jnp.abs(xs))
    ls = jnp.minimum(xs, 0.0) - _log2(1.0 + e)
    ls_neg = ls - xs
    if lb is None:
        return ls_neg, ls
    a = _log2(lb)
    rest = _log2(1.0 - lb)
    b = rest + ls
    mx = jnp.maximum(a, b)
    log_f = mx + _log2(jnp.exp2(a - mx) + jnp.exp2(b - mx))
    return rest + ls_neg, log_f


def _gla_chunk(q, x, v, lb, consts, reverse):
    tri, sel, blk_eq, row, keeps = consts
    cs, sub = GLA_CHUNK, GLA_SUB
    nsub = cs // sub
    log_k, g = _gla_gates(x, lb)
    yield None
    b = jnp.dot(tri, g, precision=lax.Precision.HIGHEST, preferred_element_type=F32)
    yield None
    edge = b[0:1] if reverse else b[cs - 1:cs]
    vb = v.astype(BF16)
    beta = b - log_k

    q_dec = (q * jnp.exp2(b)).astype(BF16)
    k_edge = jnp.exp2(edge - beta).astype(BF16)
    decay = jnp.exp2(edge)

    qs, ks = [], []
    for j in range(nsub):
        if (reverse and j == 0) or (not reverse and j == nsub - 1):
            continue
        lo, hi = j * sub, (j + 1) * sub
        anchor = b[lo:lo + 1] if reverse else b[hi - 1:hi]
        q_ok = (row < lo) if reverse else (row >= hi)
        k_ok = jnp.logical_and(row >= lo, row < hi)
        qs.append((q * jnp.exp2(jnp.where(q_ok, b - anchor, MASKED_EXP))).astype(BF16))
        ks.append(jnp.exp2(jnp.where(k_ok, anchor - beta, MASKED_EXP)).astype(BF16))
    yield None

    cols = []
    for s in range(sub):
        keep = keeps[s]
        parts = []
        for j in range(nsub):
            lo = j * sub
            bb = b[lo:lo + sub]
            w = jnp.exp2(jnp.where(keep, bb - beta[lo + s:lo + s + 1], MASKED_EXP))
            parts.append((q[lo:lo + sub] * w).astype(BF16))
        cols.append(jnp.concatenate(parts, axis=0))
    yield None

    a_mat = _dot_nt(jnp.concatenate(qs, axis=1), jnp.concatenate(ks, axis=1))
    diag = _dot(jnp.concatenate(cols, axis=1), sel)
    delta = _dot_tn(vb, k_edge)
    yield None

    a_mat = a_mat + jnp.where(blk_eq, diag, 0.0)
    o_intra = _dot(a_mat.astype(BF16), vb)
    yield o_intra, q_dec, decay, delta


def _gla_body(q_ref, ff_ref, fb_ref, i_ref, g_ref, lbraw_ref, ng_ref, o_ref,
              of_ref, ob_ref, st_ref, *, layer):
    seq = q_ref.shape[0]
    cs, sub = GLA_CHUNK, GLA_SUB
    nc = seq // cs

    if layer == 0:
        lbs = (None, None)
    else:
        raw = lbraw_ref[...]
        e = jnp.exp(raw - jnp.max(raw, axis=0, keepdims=True))
        p = e / jnp.sum(e, axis=0, keepdims=True)
        lb2 = jnp.sum(p[1:layer + 1], axis=0)
        lbs = (lb2[0:1], lb2[1:2])

    r_i = lax.broadcasted_iota(jnp.int32, (cs, cs), 0)
    c_i = lax.broadcasted_iota(jnp.int32, (cs, cs), 1)
    tri_f = (c_i <= r_i).astype(F32)
    tri_b = (c_i >= r_i).astype(F32)
    blk_eq = (r_i // sub) == (c_i // sub)
    sr = lax.broadcasted_iota(jnp.int32, (sub * HG_DK, cs), 0)
    sc = lax.broadcasted_iota(jnp.int32, (sub * HG_DK, cs), 1)
    sel = ((sr // HG_DK) == (sc % sub)).astype(BF16)
    row = lax.broadcasted_iota(jnp.int32, (cs, 1), 0)

    st_ref[...] = jnp.zeros_like(st_ref)

    group = min(GLA_GROUP, nc)
    sub_row = lax.broadcasted_iota(jnp.int32, (sub, HG_DK), 0)
    keeps_f = [sub_row >= s for s in range(sub)]
    keeps_b = [sub_row <= s for s in range(sub)]
    directions = ((ff_ref, lbs[0], (tri_f, sel, blk_eq, row, keeps_f), False, of_ref),
                  (fb_ref, lbs[1], (tri_b, sel, blk_eq, row, keeps_b), True, ob_ref))

    def step(i, carry):
        chains, slots = [], []
        for u in range(group):
            c = i * group + u
            for d, (f_ref, lb, consts, reverse, out_ref) in enumerate(directions):
                r = pl.ds(pl.multiple_of((nc - 1 - c if reverse else c) * cs, cs), cs)
                chains.append(_gla_chunk(q_ref[r, :].astype(F32), f_ref[r, :].astype(F32),
                                         i_ref[r, :].astype(F32), lb, consts, reverse))
                slots.append((d, r, out_ref))
        for results in zip(*chains):
            pass
        states = [st_ref[0], st_ref[1]]
        for (d, r, out_ref), (o_intra, q_dec, decay, delta) in zip(slots, results):
            out_ref[r, :] = o_intra + _dot_nt(q_dec, states[d].astype(BF16))
            states[d] = decay * states[d] + delta
        st_ref[0] = states[0]
        st_ref[1] = states[1]
        return carry

    lax.fori_loop(0, nc // group, step, 0)

    tr = min(512, seq)
    ng = ng_ref[...]

    def fin(i, carry):
        r = pl.ds(pl.multiple_of(i * tr, tr), tr)
        o = of_ref[r, :] + ob_ref[r, :]
        ms = jnp.mean(o * o, axis=-1, keepdims=True)
        y = o * lax.rsqrt(ms + EPS) * ng
        gate = g_ref[r, :].astype(F32)
        o_ref[r, :] = (y * gate * _sigmoid(gate)).astype(o_ref.dtype)
        return carry

    lax.fori_loop(0, seq // tr, fin, 0)


def _hgrn2(proj3, lb_raw, norm_g, layer):
    bsz, seq, _ = proj3.shape

    def col(base):
        return pl.BlockSpec((None, seq, HG_DK), lambda b, h: (b, 0, base // HG_DK + h))

    return pl.pallas_call(
        functools.partial(_gla_body, layer=layer),
        out_shape=jax.ShapeDtypeStruct((bsz, seq, HG_WIDTH), BF16),
        grid=(bsz, HG_HEADS),
        in_specs=[col(COL_AQ), col(COL_AFF), col(COL_AFB), col(COL_AI), col(COL_AG),
                  pl.BlockSpec((DEPTH, 2, HG_DK), lambda b, h: (0, 0, h)),
                  pl.BlockSpec((1, HG_DK), lambda b, h: (0, h))],
        out_specs=pl.BlockSpec((None, seq, HG_DK), lambda b, h: (b, 0, h)),
        scratch_shapes=[pltpu.VMEM((seq, HG_DK), F32),
                        pltpu.VMEM((seq, HG_DK), F32),
                        pltpu.VMEM((2, HG_DK, HG_DK), F32)],
        compiler_params=_params(("parallel", "parallel")),
        name="hgrn2",
    )(proj3, proj3, proj3, proj3, proj3, lb_raw, norm_g)


def _gelu(x):
    return 0.5 * x * (1.0 + jnp.tanh(math.sqrt(2.0 / math.pi) * (x + 0.044715 * (x * x * x))))


def _sgu_body(u_ref, v_ref, ng_ref, nb_ref, ws_ref, bs_ref, o_ref):
    tm = u_ref.shape[0]
    v = _gelu(v_ref[...].astype(F32))
    mu = jnp.mean(v, axis=-1, keepdims=True)
    d = v - mu
    var = jnp.mean(d * d, axis=-1, keepdims=True)
    vn = (d * lax.rsqrt(var + EPS) * ng_ref[...] + nb_ref[...]).astype(BF16)
    bs = bs_ref[...]
    for c in range(tm // SG_CHUNK):
        r0 = c * SG_CHUNK
        for g in range(SG_GROUPS):
            c0 = g * SG_GDIM
            mixed = _dot(ws_ref[g], vn[r0:r0 + SG_CHUNK, c0:c0 + SG_GDIM]) + bs[:, g:g + 1]
            u = _gelu(u_ref[r0:r0 + SG_CHUNK, c0:c0 + SG_GDIM].astype(F32))
            o_ref[r0:r0 + SG_CHUNK, c0:c0 + SG_GDIM] = (u * mixed).astype(o_ref.dtype)


def _spatial_gating(proj, norm_g, norm_b, w_s, b_s_t, *, tm=256):
    m = proj.shape[0]
    return pl.pallas_call(
        _sgu_body,
        out_shape=jax.ShapeDtypeStruct((m, SG_WIDTH), BF16),
        grid=(m // tm,),
        in_specs=[pl.BlockSpec((tm, SG_WIDTH), lambda i: (i, REST_CU // SG_WIDTH)),
                  pl.BlockSpec((tm, SG_WIDTH), lambda i: (i, REST_CV // SG_WIDTH)),
                  pl.BlockSpec((1, SG_WIDTH), lambda i: (0, 0)),
                  pl.BlockSpec((1, SG_WIDTH), lambda i: (0, 0)),
                  pl.BlockSpec((SG_GROUPS, SG_CHUNK, SG_CHUNK), lambda i: (0, 0, 0)),
                  pl.BlockSpec((SG_CHUNK, SG_GROUPS), lambda i: (0, 0))],
        out_specs=pl.BlockSpec((tm, SG_WIDTH), lambda i: (i, 0)),
        compiler_params=_params(("parallel",)),
        name="spatial_gating",
    )(proj, proj, norm_g, norm_b, w_s, b_s_t)


def _rotary_tables(seq):
    half = ROT_DIM // 2
    pos = jnp.arange(seq, dtype=F32)
    freqs = ROPE_THETA ** (-jnp.arange(0, ROT_DIM, 2, dtype=F32) / ROT_DIM)
    ang = pos[:, None] * freqs[None, :]
    cos, sin = jnp.cos(ang), jnp.sin(ang)
    rest = LANES - ROT_DIM
    c = jnp.concatenate([cos, cos, jnp.ones((seq, rest), F32)], axis=1)
    s1 = jnp.concatenate([jnp.zeros((seq, half), F32), sin, jnp.zeros((seq, rest), F32)], axis=1)
    s2 = jnp.concatenate([-sin, jnp.zeros((seq, LANES - half), F32)], axis=1)
    q_scale = LOG2E * DA_DH ** -0.5
    return tuple(jnp.stack([t * q_scale, t]) for t in (c, s1, s2))


def kernel(x, w_in, hg_lb_raw, hg_norm_g, da_lambda, da_norm_g, sg_norm_g, sg_norm_b, sg_w_s, sg_b_s, w_branch_a, w_branch_b, w_branch_c, w_gate, b_gate, w_out, ln1_g, ln1_b, w_up, w_down, ln2_g, ln2_b):
    bsz, seq, d = x.shape
    m = bsz * seq
    rot_c, rot_s1, rot_s2 = _rotary_tables(seq)
    xf = x.reshape(m, d)
    xb = xf.astype(BF16)
    for layer in range(DEPTH):
        qk = _proj_qk(xb, w_in, layer, rot_c, rot_s1, rot_s2, seq)
        rest = _proj_rest(xb, w_in, layer)
        gates = _gate_proj(xb, w_gate, layer, b_gate[layer].reshape(1, -1))
        qk3 = qk.reshape(bsz, seq, 2 * DA_QK_WIDTH)
        rest3 = rest.reshape(bsz, seq, REST_WIDTH)
        y_a = _hgrn2(rest3, hg_lb_raw, hg_norm_g[layer].reshape(1, -1), layer)
        y_b = _attention(qk3, rest3, da_lambda[layer], da_norm_g[layer].reshape(-1, 1), layer)
        y_c = _spatial_gating(rest, sg_norm_g[layer].reshape(1, -1), sg_norm_b[layer].reshape(1, -1),
                              sg_w_s[layer].astype(BF16), sg_b_s[layer].T)
        merged, w_out_b = _merge(y_a.reshape(m, HG_WIDTH), y_b.reshape(m, DA_V_WIDTH), y_c,
                                 w_branch_a, w_branch_b, w_branch_c, layer, gates, w_out)
        xf, xb = _mm_ln(merged, w_out_b, 0, xf, ln1_g[layer].reshape(1, -1),
                        ln1_b[layer].reshape(1, -1), name="out_proj_ln")
        hid, w_down_b = _up_proj(xb, w_up, layer, w_down)
        xf, xb = _mm_ln(hid, w_down_b, 0, xf, ln2_g[layer].reshape(1, -1),
                        ln2_b[layer].reshape(1, -1), name="down_proj_ln")
    return xf.reshape(bsz, seq, d)
```

```python
import functools
import math

import jax
import jax.numpy as jnp
from jax import lax
from jax.experimental import pallas as pl
from jax.experimental.pallas import tpu as pltpu

D_MODEL = 4096
DEPTH = 2
HG_WIDTH = D_MODEL // 4
HG_DK = 128
HG_HEADS = HG_WIDTH // HG_DK
DA_DH = 128
DA_HEADS = D_MODEL // (4 * DA_DH)
DA_QK_WIDTH = DA_HEADS * 2 * DA_DH
DA_V_WIDTH = DA_HEADS * 2 * DA_DH
ROPE_THETA = 500000.0
ROT_DIM = DA_DH // 4
SG_CHUNK = 128
SG_WIDTH = D_MODEL // 4
SG_GDIM = 128
SG_GROUPS = SG_WIDTH // SG_GDIM
N_BRANCH = 3
D_FF = 4 * D_MODEL
ALPHA = (2.0 * DEPTH) ** 0.25
EPS = 1e-5
D_IN = 5 * HG_WIDTH + 2 * DA_QK_WIDTH + DA_V_WIDTH + 2 * SG_WIDTH

COL_AQ, COL_AFF, COL_AFB, COL_AI, COL_AG = (i * HG_WIDTH for i in range(5))
COL_BQ = 5 * HG_WIDTH
COL_BK = COL_BQ + DA_QK_WIDTH
COL_BV = COL_BK + DA_QK_WIDTH
COL_CU = COL_BV + DA_V_WIDTH
COL_CV = COL_CU + SG_WIDTH

QK_Q, QK_K = 0, DA_QK_WIDTH
REST_WIDTH = D_IN - 2 * DA_QK_WIDTH
REST_BV = COL_BV - 2 * DA_QK_WIDTH
REST_CU = COL_CU - 2 * DA_QK_WIDTH
REST_CV = COL_CV - 2 * DA_QK_WIDTH

LANES = 128
VMEM_LIMIT = 56 * 1024 * 1024
GLA_CHUNK = 64
GLA_SUB = 16
LN_ROWS = 64
ATT_STRIP = 256
GLA_GROUP = 8
LOG2E = math.log2(math.e)
MASKED_EXP = -1e30

F32 = jnp.float32
BF16 = jnp.bfloat16


def _params(sem):
    return pltpu.CompilerParams(dimension_semantics=sem, vmem_limit_bytes=VMEM_LIMIT)


def _dot(a, b):
    return jnp.dot(a, b, preferred_element_type=F32)


def _dot_nt(a, b):
    return lax.dot_general(a, b, (((1,), (1,)), ((), ())), preferred_element_type=F32)


def _dot_tn(a, b):
    return lax.dot_general(a, b, (((0,), (0,)), ((), ())), preferred_element_type=F32)


def _sigmoid(x):
    return 0.5 * jnp.tanh(0.5 * x) + 0.5


def _proj_qk_body(x_ref, w_ref, c_ref, s1_ref, s2_ref, o_ref, acc_a, acc_b, *, steps):
    t = pl.program_id(0)

    def rotary(prev_ref):
        c = c_ref[...]
        s1 = s1_ref[...]
        s2 = s2_ref[...]
        for g in range(o_ref.shape[1] // LANES):
            v = prev_ref[:, g * LANES:(g + 1) * LANES]
            r = (v * c + pltpu.roll(v, ROT_DIM // 2, 1) * s1
                 + pltpu.roll(v, LANES - ROT_DIM // 2, 1) * s2)
            o_ref[:, g * LANES:(g + 1) * LANES] = r.astype(o_ref.dtype)

    @pl.when(t == 0)
    def _():
        acc_b[...] = jnp.zeros_like(acc_b)

    for parity, (cur_ref, prev_ref) in enumerate(((acc_a, acc_b), (acc_b, acc_a))):
        @pl.when(jnp.logical_and(t < steps, t % 2 == parity))
        def _(cur_ref=cur_ref, prev_ref=prev_ref):
            cur_ref[...] = _dot(x_ref[...], w_ref[...].astype(BF16))
            rotary(prev_ref)

    @pl.when(t == steps)
    def _():
        rotary(acc_a if (steps - 1) % 2 == 0 else acc_b)


def _proj_qk(h, w, layer, rot_c, rot_s1, rot_s2, seq, *, tm=1024, tn=512):
    m, k = h.shape
    tm = min(tm, seq)
    nseq = seq // tm
    nq = DA_QK_WIDTH // tn
    ncol = 2 * nq
    first = COL_BQ // tn
    steps = (m // tm) * ncol

    def cur(t):
        tc = jnp.minimum(t, steps - 1)
        return tc // ncol, tc % ncol

    def lag(t):
        tl = jnp.maximum(t - 1, 0)
        return tl // ncol, tl % ncol

    tab = pl.BlockSpec((None, tm, LANES), lambda t: (lag(t)[1] // nq, lag(t)[0] % nseq, 0))
    return pl.pallas_call(
        functools.partial(_proj_qk_body, steps=steps),
        out_shape=jax.ShapeDtypeStruct((m, 2 * DA_QK_WIDTH), BF16),
        grid=(steps + 1,),
        in_specs=[pl.BlockSpec((tm, k), lambda t: (cur(t)[0], 0)),
                  pl.BlockSpec((None, k, tn), lambda t: (layer, 0, first + cur(t)[1])),
                  tab, tab, tab],
        out_specs=pl.BlockSpec((tm, tn), lag),
        scratch_shapes=[pltpu.VMEM((tm, tn), F32), pltpu.VMEM((tm, tn), F32)],
        compiler_params=_params(("arbitrary",)),
        name="proj_qk",
    )(h, w, rot_c, rot_s1, rot_s2)


def _proj_rest_body(x_ref, w_ref, o_ref):
    o_ref[...] = _dot(x_ref[...], w_ref[...].astype(BF16)).astype(o_ref.dtype)


def _proj_rest(h, w, layer, *, tm=1024, tn=512):
    m, k = h.shape
    tm = min(tm, m)
    first_skip = COL_BQ // tn
    skipped = 2 * DA_QK_WIDTH // tn
    return pl.pallas_call(
        _proj_rest_body,
        out_shape=jax.ShapeDtypeStruct((m, REST_WIDTH), BF16),
        grid=(m // tm, REST_WIDTH // tn),
        in_specs=[pl.BlockSpec((tm, k), lambda i, j: (i, 0)),
                  pl.BlockSpec((None, k, tn),
                               lambda i, j: (layer, 0, j + jnp.where(j >= first_skip, skipped, 0)))],
        out_specs=pl.BlockSpec((tm, tn), lambda i, j: (i, j)),
        compiler_params=_params(("parallel", "arbitrary")),
        name="proj_rest",
    )(h, w)


def _gate_body(x_ref, w_ref, b_ref, o_ref):
    acc = _dot(x_ref[...], w_ref[...].astype(BF16)) + b_ref[...]
    o_ref[...] = _sigmoid(acc).astype(o_ref.dtype)


def _gate_proj(h, w, layer, b, *, tm=1024, tn=512):
    m, k = h.shape
    n = w.shape[2]
    tm = min(tm, m)
    return pl.pallas_call(
        _gate_body,
        out_shape=jax.ShapeDtypeStruct((m, n), BF16),
        grid=(m // tm, n // tn),
        in_specs=[pl.BlockSpec((tm, k), lambda i, j: (i, 0)),
                  pl.BlockSpec((None, k, tn), lambda i, j: (layer, 0, j)),
                  pl.BlockSpec((1, tn), lambda i, j: (0, j))],
        out_specs=pl.BlockSpec((tm, tn), lambda i, j: (i, j)),
        compiler_params=_params(("parallel", "arbitrary")),
        name="gate_proj",
    )(h, w, b)


def _up_body(x_ref, w_ref, o_ref):
    r = jnp.maximum(_dot(x_ref[...], w_ref[...].astype(BF16)), 0.0)
    o_ref[...] = (r * r).astype(o_ref.dtype)


def _up_proj(h, w, layer, *, tm=1024, tn=512):
    m, k = h.shape
    n = w.shape[2]
    tm = min(tm, m)
    return pl.pallas_call(
        _up_body,
        out_shape=jax.ShapeDtypeStruct((m, n), BF16),
        grid=(m // tm, n // tn),
        in_specs=[pl.BlockSpec((tm, k), lambda i, j: (i, 0)),
                  pl.BlockSpec((None, k, tn), lambda i, j: (layer, 0, j))],
        out_specs=pl.BlockSpec((tm, tn), lambda i, j: (i, j)),
        compiler_params=_params(("parallel", "arbitrary")),
        name="up_proj",
    )(h, w)


def _mm_ln_body(y_ref, w_ref, res_ref, g_ref, b_ref, of_ref, ob_ref, acc_a, acc_b, *, nk, nt):
    i = pl.program_id(0)
    kk = pl.program_id(1)
    slab = res_ref.shape[0]
    rows = min(slab, LN_ROWS)

    def normalise(prev_ref):
        g = g_ref[...]
        b = b_ref[...]
        for r0 in range(0, slab, rows):
            rs = pl.ds(pl.multiple_of(kk * slab + r0, rows), rows)
            t = prev_ref[rs, :] + ALPHA * res_ref[r0:r0 + rows, :]
            prev_ref[rs, :] = jnp.zeros((rows, prev_ref.shape[1]), F32)
            mu = jnp.mean(t, axis=-1, keepdims=True)
            d = t - mu
            var = jnp.mean(d * d, axis=-1, keepdims=True)
            out = d * lax.rsqrt(var + EPS) * g + b
            of_ref[r0:r0 + rows, :] = out
            ob_ref[r0:r0 + rows, :] = out.astype(ob_ref.dtype)

    @pl.when(jnp.logical_and(i == 0, kk == 0))
    def _():
        acc_a[...] = jnp.zeros_like(acc_a)
        acc_b[...] = jnp.zeros_like(acc_b)

    for parity, (cur_ref, prev_ref) in enumerate(((acc_a, acc_b), (acc_b, acc_a))):
        @pl.when(jnp.logical_and(i < nt, i % 2 == parity))
        def _(cur_ref=cur_ref, prev_ref=prev_ref):
            cur_ref[...] += _dot(y_ref[...], w_ref[...])
            normalise(prev_ref)

    @pl.when(i == nt)
    def _():
        normalise(acc_a if (nt - 1) % 2 == 0 else acc_b)


def _mm_ln(y, w, layer, res, g, b, *, tm=512, tk=1024, name):
    m, k = y.shape
    n = w.shape[2]
    tm = min(tm, m)
    nk = k // tk
    nt = m // tm
    slab = tm // nk

    def lagged(i, kk):
        return (jnp.where(i == 0, 0, (i - 1) * nk + kk), 0)

    def last_k(i, kk):
        return jnp.where(i < nt, kk, nk - 1)

    return pl.pallas_call(
        functools.partial(_mm_ln_body, nk=nk, nt=nt),
        out_shape=(jax.ShapeDtypeStruct((m, n), F32), jax.ShapeDtypeStruct((m, n), BF16)),
        grid=(nt + 1, nk),
        in_specs=[pl.BlockSpec((tm, tk), lambda i, kk: (jnp.minimum(i, nt - 1), last_k(i, kk))),
                  pl.BlockSpec((None, tk, n), lambda i, kk: (layer, last_k(i, kk), 0)),
                  pl.BlockSpec((slab, n), lagged),
                  pl.BlockSpec((1, n), lambda i, kk: (0, 0)),
                  pl.BlockSpec((1, n), lambda i, kk: (0, 0))],
        out_specs=(pl.BlockSpec((slab, n), lagged),
                   pl.BlockSpec((slab, n), lagged)),
        scratch_shapes=[pltpu.VMEM((tm, n), F32), pltpu.VMEM((tm, n), F32)],
        compiler_params=_params(("arbitrary", "arbitrary")),
        name=name,
    )(y, w, res, g, b)


def _rider_specs(rider, layer, steps, step_of):
    _, r, c = rider.shape
    slab = r // steps
    assert slab * steps == r and slab % 16 == 0
    return (pl.BlockSpec((None, slab, c), lambda *g: (layer, step_of(*g), 0)),
            pl.BlockSpec((None, slab, c), lambda *g: (0, step_of(*g), 0)),
            jax.ShapeDtypeStruct((1, r, c), BF16))


def _merge_body(ya_ref, yb_ref, yc_ref, wa_ref, wb_ref, wc_ref, ga_ref, gb_ref, gc_ref, o_ref):
    acc = ga_ref[...].astype(F32) * _dot(ya_ref[...], wa_ref[...].astype(BF16))
    acc += gb_ref[...].astype(F32) * _dot(yb_ref[...], wb_ref[...].astype(BF16))
    acc += gc_ref[...].astype(F32) * _dot(yc_ref[...], wc_ref[...].astype(BF16))
    o_ref[...] = acc.astype(o_ref.dtype)


def _merge(ya, yb, yc, wa, wb, wc, layer, gates, *, tm=1024, tn=512):
    m = ya.shape[0]
    n = wa.shape[2]
    tm = min(tm, m)
    nb = n // tn

    def act(width):
        return pl.BlockSpec((tm, width), lambda i, j: (i, 0))

    def wt(width):
        return pl.BlockSpec((None, width, tn), lambda i, j: (layer, 0, j))

    def gate(branch):
        return pl.BlockSpec((tm, tn), lambda i, j: (i, branch * nb + j))

    return pl.pallas_call(
        _merge_body,
        out_shape=jax.ShapeDtypeStruct((m, n), BF16),
        grid=(m // tm, nb),
        in_specs=[act(ya.shape[1]), act(yb.shape[1]), act(yc.shape[1]),
                  wt(wa.shape[1]), wt(wb.shape[1]), wt(wc.shape[1]),
                  gate(0), gate(1), gate(2)],
        out_specs=pl.BlockSpec((tm, tn), lambda i, j: (i, j)),
        compiler_params=_params(("parallel", "arbitrary")),
        name="merge",
    )(ya, yb, yc, wa, wb, wc, gates, gates, gates)


def _attn_body(q_ref, k_ref, v_ref, lam_ref, g_ref, ra_ref, rb_ref, o_ref, rab_ref, rbb_ref,
               qt_ref, vt_ref, m_ref, l_ref, acc_ref, *, tk, lam_init):
    rab_ref[...] = ra_ref[...].astype(rab_ref.dtype)
    rbb_ref[...] = rb_ref[...].astype(rbb_ref.dtype)
    seq = k_ref.shape[0]
    tq = q_ref.shape[0]
    nkv = seq // tk

    @pl.when(pl.program_id(2) == 0)
    def _():
        def transpose_values(i, carry):
            r = pl.ds(pl.multiple_of(i * tk, tk), tk)
            vt_ref[i] = v_ref[r, :].astype(F32).T.astype(BF16)
            return carry

        lax.fori_loop(0, nkv, transpose_values, 0)

    qt_ref[...] = q_ref[...].astype(F32).T.astype(BF16)
    m_ref[...] = jnp.full_like(m_ref, -jnp.inf)
    l_ref[...] = jnp.zeros_like(l_ref)
    acc_ref[...] = jnp.zeros_like(acc_ref)

    def step(i, carry):
        rows = pl.ds(pl.multiple_of(i * tk, tk), tk)
        vt = vt_ref[i]
        for j in range(tq // ATT_STRIP):
            for c in range(2):
                kc = k_ref[rows, c * DA_DH:(c + 1) * DA_DH]
                cols = slice(j * ATT_STRIP, (j + 1) * ATT_STRIP)
                st = _dot(kc, qt_ref[c * DA_DH:(c + 1) * DA_DH, cols])
                m_old = m_ref[c, :, cols]
                m_new = jnp.maximum(m_old, jnp.max(st, axis=0, keepdims=True))
                alpha = jnp.exp2(m_old - m_new)
                p = jnp.exp2(st - m_new)
                l_ref[c, :, cols] = alpha * l_ref[c, :, cols] + jnp.sum(p, axis=0, keepdims=True)
                acc_ref[c, :, cols] = alpha * acc_ref[c, :, cols] + _dot(vt, p.astype(BF16))
                m_ref[c, :, cols] = m_new
        return carry

    lax.fori_loop(0, nkv, step, 0, unroll=16)

    lp = lam_ref[...]
    lam = (jnp.exp(jnp.sum(lp[0:1] * lp[1:2], axis=-1, keepdims=True))
           - jnp.exp(jnp.sum(lp[2:3] * lp[3:4], axis=-1, keepdims=True)) + lam_init)
    o = acc_ref[0] * (1.0 / l_ref[0]) - acc_ref[1] * (lam / l_ref[1])
    ms = jnp.mean(o * o, axis=0, keepdims=True)
    o = o * (lax.rsqrt(ms + EPS) * (1.0 - lam_init)) * g_ref[...]
    o_ref[...] = o.T.astype(o_ref.dtype)


def _attention(qk3, rest3, lam_params, norm_g_col, layer, riders, *, tq=1024, tk=256):
    bsz, seq, _ = qk3.shape
    tq = min(tq, seq)
    tk = min(tk, seq)
    nq = seq // tq
    hw = 2 * DA_DH
    lam_init = 0.8 - 0.6 * math.exp(-0.3 * layer)
    steps = bsz * DA_HEADS * nq
    ra, rb = (_rider_specs(r, layer, steps, lambda b, h, i: (b * DA_HEADS + h) * nq + i) for r in riders)
    return pl.pallas_call(
        functools.partial(_attn_body, tk=tk, lam_init=lam_init),
        out_shape=(jax.ShapeDtypeStruct((bsz, seq, DA_V_WIDTH), BF16), ra[2], rb[2]),
        grid=(bsz, DA_HEADS, nq),
        in_specs=[pl.BlockSpec((None, tq, hw), lambda b, h, i: (b, i, QK_Q // hw + h)),
                  pl.BlockSpec((None, seq, hw), lambda b, h, i: (b, 0, QK_K // hw + h)),
                  pl.BlockSpec((None, seq, hw), lambda b, h, i: (b, 0, REST_BV // hw + h)),
                  pl.BlockSpec((4, DA_DH), lambda b, h, i: (0, 0)),
                  pl.BlockSpec((hw, 1), lambda b, h, i: (0, 0)),
                  ra[0], rb[0]],
        out_specs=(pl.BlockSpec((None, tq, hw), lambda b, h, i: (b, i, h)), ra[1], rb[1]),
        scratch_shapes=[pltpu.VMEM((hw, tq), BF16),
                        pltpu.VMEM((seq // tk, hw, tk), BF16),
                        pltpu.VMEM((2, 1, tq), F32),
                        pltpu.VMEM((2, 1, tq), F32),
                        pltpu.VMEM((2, hw, tq), F32)],
        compiler_params=_params(("arbitrary", "arbitrary", "arbitrary")),
        name="diff_attention",
    )(qk3, qk3, rest3, lam_params, norm_g_col, *riders)


def _log2(x):
    return jnp.log(x) * LOG2E


def _gla_gates(x, lb):
    xs = x * LOG2E
    e = jnp.exp2(-jnp.abs(xs))
    ls = jnp.minimum(xs, 0.0) - _log2(1.0 + e)
    ls_neg = ls - xs
    if lb is None:
        return ls_neg, ls
    a = _log2(lb)
    rest = _log2(1.0 - lb)
    b = rest + ls
    mx = jnp.maximum(a, b)
    log_f = mx + _log2(jnp.exp2(a - mx) + jnp.exp2(b - mx))
    return rest + ls_neg, log_f


def _gla_chunk(q, x, v, lb, consts, reverse):
    tri, sel, blk_eq, row, keeps = consts
    cs, sub = GLA_CHUNK, GLA_SUB
    nsub = cs // sub
    log_k, g = _gla_gates(x, lb)
    yield None
    b = jnp.dot(tri, g, precision=lax.Precision.HIGHEST, preferred_element_type=F32)
    yield None
    edge = b[0:1] if reverse else b[cs - 1:cs]
    vb = v.astype(BF16)
    beta = b - log_k

    q_dec = (q * jnp.exp2(b)).astype(BF16)
    k_edge = jnp.exp2(edge - beta).astype(BF16)
    decay = jnp.exp2(edge)

    qs, ks = [], []
    for j in range(nsub):
        if (reverse and j == 0) or (not reverse and j == nsub - 1):
            continue
        lo, hi = j * sub, (j + 1) * sub
        anchor = b[lo:lo + 1] if reverse else b[hi - 1:hi]
        q_ok = (row < lo) if reverse else (row >= hi)
        k_ok = jnp.logical_and(row >= lo, row < hi)
        qs.append((q * jnp.exp2(jnp.where(q_ok, b - anchor, MASKED_EXP))).astype(BF16))
        ks.append(jnp.exp2(jnp.where(k_ok, anchor - beta, MASKED_EXP)).astype(BF16))
    yield None

    cols = []
    for s in range(sub):
        keep = keeps[s]
        parts = []
        for j in range(nsub):
            lo = j * sub
            bb = b[lo:lo + sub]
            w = jnp.exp2(jnp.where(keep, bb - beta[lo + s:lo + s + 1], MASKED_EXP))
            parts.append((q[lo:lo + sub] * w).astype(BF16))
        cols.append(jnp.concatenate(parts, axis=0))
    yield None

    a_mat = _dot_nt(jnp.concatenate(qs, axis=1), jnp.concatenate(ks, axis=1))
    diag = _dot(jnp.concatenate(cols, axis=1), sel)
    delta = _dot_tn(vb, k_edge)
    yield None

    a_mat = a_mat + jnp.where(blk_eq, diag, 0.0)
    o_intra = _dot(a_mat.astype(BF16), vb)
    yield o_intra, q_dec, decay, delta


def _gla_body(q_ref, ff_ref, fb_ref, i_ref, g_ref, lbraw_ref, ng_ref, o_ref,
              of_ref, ob_ref, st_ref, *, layer):
    seq = q_ref.shape[0]
    cs, sub = GLA_CHUNK, GLA_SUB
    nc = seq // cs

    if layer == 0:
        lbs = (None, None)
    else:
        raw = lbraw_ref[...]
        e = jnp.exp(raw - jnp.max(raw, axis=0, keepdims=True))
        p = e / jnp.sum(e, axis=0, keepdims=True)
        lb2 = jnp.sum(p[1:layer + 1], axis=0)
        lbs = (lb2[0:1], lb2[1:2])

    r_i = lax.broadcasted_iota(jnp.int32, (cs, cs), 0)
    c_i = lax.broadcasted_iota(jnp.int32, (cs, cs), 1)
    tri_f = (c_i <= r_i).astype(F32)
    tri_b = (c_i >= r_i).astype(F32)
    blk_eq = (r_i // sub) == (c_i // sub)
    sr = lax.broadcasted_iota(jnp.int32, (sub * HG_DK, cs), 0)
    sc = lax.broadcasted_iota(jnp.int32, (sub * HG_DK, cs), 1)
    sel = ((sr // HG_DK) == (sc % sub)).astype(BF16)
    row = lax.broadcasted_iota(jnp.int32, (cs, 1), 0)

    st_ref[...] = jnp.zeros_like(st_ref)

    group = min(GLA_GROUP, nc)
    sub_row = lax.broadcasted_iota(jnp.int32, (sub, HG_DK), 0)
    keeps_f = [sub_row >= s for s in range(sub)]
    keeps_b = [sub_row <= s for s in range(sub)]
    directions = ((ff_ref, lbs[0], (tri_f, sel, blk_eq, row, keeps_f), False, of_ref),
                  (fb_ref, lbs[1], (tri_b, sel, blk_eq, row, keeps_b), True, ob_ref))

    def step(i, carry):
        chains, slots = [], []
        for u in range(group):
            c = i * group + u
            for d, (f_ref, lb, consts, reverse, out_ref) in enumerate(directions):
                r = pl.ds(pl.multiple_of((nc - 1 - c if reverse else c) * cs, cs), cs)
                chains.append(_gla_chunk(q_ref[r, :].astype(F32), f_ref[r, :].astype(F32),
                                         i_ref[r, :].astype(F32), lb, consts, reverse))
                slots.append((d, r, out_ref))
        for results in zip(*chains):
            pass
        states = [st_ref[0], st_ref[1]]
        for (d, r, out_ref), (o_intra, q_dec, decay, delta) in zip(slots, results):
            out_ref[r, :] = o_intra + _dot_nt(q_dec, states[d].astype(BF16))
            states[d] = decay * states[d] + delta
        st_ref[0] = states[0]
        st_ref[1] = states[1]
        return carry

    lax.fori_loop(0, nc // group, step, 0)

    tr = min(512, seq)
    ng = ng_ref[...]

    def fin(i, carry):
        r = pl.ds(pl.multiple_of(i * tr, tr), tr)
        o = of_ref[r, :] + ob_ref[r, :]
        ms = jnp.mean(o * o, axis=-1, keepdims=True)
        y = o * lax.rsqrt(ms + EPS) * ng
        gate = g_ref[r, :].astype(F32)
        o_ref[r, :] = (y * gate * _sigmoid(gate)).astype(o_ref.dtype)
        return carry

    lax.fori_loop(0, seq // tr, fin, 0)


def _hgrn2(proj3, lb_raw, norm_g, layer):
    bsz, seq, _ = proj3.shape

    def col(base):
        return pl.BlockSpec((None, seq, HG_DK), lambda b, h: (b, 0, base // HG_DK + h))

    return pl.pallas_call(
        functools.partial(_gla_body, layer=layer),
        out_shape=jax.ShapeDtypeStruct((bsz, seq, HG_WIDTH), BF16),
        grid=(bsz, HG_HEADS),
        in_specs=[col(COL_AQ), col(COL_AFF), col(COL_AFB), col(COL_AI), col(COL_AG),
                  pl.BlockSpec((DEPTH, 2, HG_DK), lambda b, h: (0, 0, h)),
                  pl.BlockSpec((1, HG_DK), lambda b, h: (0, h))],
        out_specs=pl.BlockSpec((None, seq, HG_DK), lambda b, h: (b, 0, h)),
        scratch_shapes=[pltpu.VMEM((seq, HG_DK), F32),
                        pltpu.VMEM((seq, HG_DK), F32),
                        pltpu.VMEM((2, HG_DK, HG_DK), F32)],
        compiler_params=_params(("parallel", "parallel")),
        name="hgrn2",
    )(proj3, proj3, proj3, proj3, proj3, lb_raw, norm_g)


def _gelu(x):
    return 0.5 * x * (1.0 + jnp.tanh(math.sqrt(2.0 / math.pi) * (x + 0.044715 * (x * x * x))))


def _sgu_body(u_ref, v_ref, ng_ref, nb_ref, ws_ref, bs_ref, o_ref):
    tm = u_ref.shape[0]
    v = _gelu(v_ref[...].astype(F32))
    mu = jnp.mean(v, axis=-1, keepdims=True)
    d = v - mu
    var = jnp.mean(d * d, axis=-1, keepdims=True)
    vn = (d * lax.rsqrt(var + EPS) * ng_ref[...] + nb_ref[...]).astype(BF16)
    bs = bs_ref[...]
    for c in range(tm // SG_CHUNK):
        r0 = c * SG_CHUNK
        for g in range(SG_GROUPS):
            c0 = g * SG_GDIM
            mixed = _dot(ws_ref[g], vn[r0:r0 + SG_CHUNK, c0:c0 + SG_GDIM]) + bs[:, g:g + 1]
            u = _gelu(u_ref[r0:r0 + SG_CHUNK, c0:c0 + SG_GDIM].astype(F32))
            o_ref[r0:r0 + SG_CHUNK, c0:c0 + SG_GDIM] = (u * mixed).astype(o_ref.dtype)


def _spatial_gating(proj, norm_g, norm_b, w_s, b_s_t, *, tm=256):
    m = proj.shape[0]
    return pl.pallas_call(
        _sgu_body,
        out_shape=jax.ShapeDtypeStruct((m, SG_WIDTH), BF16),
        grid=(m // tm,),
        in_specs=[pl.BlockSpec((tm, SG_WIDTH), lambda i: (i, REST_CU // SG_WIDTH)),
                  pl.BlockSpec((tm, SG_WIDTH), lambda i: (i, REST_CV // SG_WIDTH)),
                  pl.BlockSpec((1, SG_WIDTH), lambda i: (0, 0)),
                  pl.BlockSpec((1, SG_WIDTH), lambda i: (0, 0)),
                  pl.BlockSpec((SG_GROUPS, SG_CHUNK, SG_CHUNK), lambda i: (0, 0, 0)),
                  pl.BlockSpec((SG_CHUNK, SG_GROUPS), lambda i: (0, 0))],
        out_specs=pl.BlockSpec((tm, SG_WIDTH), lambda i: (i, 0)),
        compiler_params=_params(("parallel",)),
        name="spatial_gating",
    )(proj, proj, norm_g, norm_b, w_s, b_s_t)


def _rotary_tables(seq):
    half = ROT_DIM // 2
    pos = jnp.arange(seq, dtype=F32)
    freqs = ROPE_THETA ** (-jnp.arange(0, ROT_DIM, 2, dtype=F32) / ROT_DIM)
    ang = pos[:, None] * freqs[None, :]
    cos, sin = jnp.cos(ang), jnp.sin(ang)
    rest = LANES - ROT_DIM
    c = jnp.concatenate([cos, cos, jnp.ones((seq, rest), F32)], axis=1)
    s1 = jnp.concatenate([jnp.zeros((seq, half), F32), sin, jnp.zeros((seq, rest), F32)], axis=1)
    s2 = jnp.concatenate([-sin, jnp.zeros((seq, LANES - half), F32)], axis=1)
    q_scale = LOG2E * DA_DH ** -0.5
    return tuple(jnp.stack([t * q_scale, t]) for t in (c, s1, s2))


def kernel(x, w_in, hg_lb_raw, hg_norm_g, da_lambda, da_norm_g, sg_norm_g, sg_norm_b, sg_w_s, sg_b_s, w_branch_a, w_branch_b, w_branch_c, w_gate, b_gate, w_out, ln1_g, ln1_b, w_up, w_down, ln2_g, ln2_b):
    bsz, seq, d = x.shape
    m = bsz * seq
    rot_c, rot_s1, rot_s2 = _rotary_tables(seq)
    xf = x.reshape(m, d)
    xb = xf.astype(BF16)
    for layer in range(DEPTH):
        qk = _proj_qk(xb, w_in, layer, rot_c, rot_s1, rot_s2, seq)
        rest = _proj_rest(xb, w_in, layer)
        gates = _gate_proj(xb, w_gate, layer, b_gate[layer].reshape(1, -1))
        qk3 = qk.reshape(bsz, seq, 2 * DA_QK_WIDTH)
        rest3 = rest.reshape(bsz, seq, REST_WIDTH)
        y_a = _hgrn2(rest3, hg_lb_raw, hg_norm_g[layer].reshape(1, -1), layer)
        y_b, w_out_b, w_down_b = _attention(qk3, rest3, da_lambda[layer], da_norm_g[layer].reshape(-1, 1),
                                            layer, (w_out, w_down))
        y_c = _spatial_gating(rest, sg_norm_g[layer].reshape(1, -1), sg_norm_b[layer].reshape(1, -1),
                              sg_w_s[layer].astype(BF16), sg_b_s[layer].T)
        merged = _merge(y_a.reshape(m, HG_WIDTH), y_b.reshape(m, DA_V_WIDTH), y_c,
                        w_branch_a, w_branch_b, w_branch_c, layer, gates)
        xf, xb = _mm_ln(merged, w_out_b, 0, xf, ln1_g[layer].reshape(1, -1),
                        ln1_b[layer].reshape(1, -1), name="out_proj_ln")
        hid = _up_proj(xb, w_up, layer)
        xf, xb = _mm_ln(hid, w_down_b, 0, xf, ln2_g[layer].reshape(1, -1),
                        ln2_b[layer].reshape(1, -1), name="down_proj_ln")
    return xf.reshape(bsz, seq, d)
```
